```python
import jax
import jax.numpy as jnp
from jax import lax
import numpy as np

D_MODEL = 1024
BATCH = 4
SEQ = 4096
DEPTH = 1

GRID_W = 64
CTX_LEN = 256
GM_WIDTH = 1024
GM_GROUPS = 8
GM_GROUP_DIM = GM_WIDTH // GM_GROUPS
GM_CHUNK = 2 * GRID_W
GDN_HEADS = 8
GDN_DK = 128
GDN_DV = 128
GDN_KD = GDN_HEADS * GDN_DK
GDN_VD = GDN_HEADS * GDN_DV
GDN_CONV = 3
GDN_CHUNK = 64
EPS = 1e-6

OFF_Q = 0
OFF_K = OFF_Q + GDN_KD
OFF_V = OFF_K + GDN_KD
OFF_A = OFF_V + GDN_VD
OFF_B = OFF_A + 2 * GDN_HEADS
OFF_ZB = OFF_B + 2 * GDN_HEADS
OFF_UA = OFF_ZB + GDN_VD
OFF_VA = OFF_UA + GM_WIDTH
OFF_ZA = OFF_VA + GM_WIDTH
OFF_G = OFF_ZA + GM_WIDTH
IN_COLS = OFF_G + 2 * D_MODEL

kernel_name = 'hybrid_gmlp_gdeltanet_prefix_dit'


def _rmsnorm(x, g):
    xf = x.astype(jnp.float32)
    y = xf * lax.rsqrt(jnp.mean(xf * xf, axis=-1, keepdims=True) + EPS)
    return (y * g.astype(jnp.float32)).astype(x.dtype)


def _layernorm(x, g, b):
    xf = x.astype(jnp.float32)
    xc = xf - jnp.mean(xf, axis=-1, keepdims=True)
    y = xc * lax.rsqrt(jnp.mean(xc * xc, axis=-1, keepdims=True) + EPS)
    return (y * g.astype(jnp.float32) + b.astype(jnp.float32)).astype(x.dtype)


def _l2norm(x):
    return x * lax.rsqrt(jnp.sum(x * x, axis=-1, keepdims=True) + EPS)


def _heads(u, d):
    return u.reshape(u.shape[:-1] + (u.shape[-1] // d, d))


def _modulation(cond, w_mod, b_mod):
    m = jax.nn.silu(cond) @ w_mod + b_mod
    m = m.reshape((-1, 1, 3, D_MODEL))
    return m[:, :, 0], m[:, :, 1], m[:, :, 2]


def _projector(h, w_in, full):
    if full:
        p = h @ w_in
        return lambda lo, hi: p[..., lo:hi]
    return lambda lo, hi: h @ w_in[:, lo:hi]


def _conv_silu(u, w):
    pad = w.shape[0] // 2
    y = lax.conv_general_dilated(u, w[:, None, :].astype(u.dtype), window_strides=(1,),
                                 padding=[(pad, pad)], dimension_numbers=('NWC', 'WIO', 'NWC'),
                                 feature_group_count=u.shape[-1])
    return jax.nn.silu(y)


def _gdn_side(col, w_conv, a_log, dt_bias, with_q):
    lo = OFF_Q if with_q else OFF_K
    qkv = _conv_silu(col(lo, OFF_A), w_conv[:, lo - OFF_Q:]).astype(jnp.float32)
    k = _l2norm(_heads(qkv[..., -(GDN_KD + GDN_VD):-GDN_VD], GDN_DK))
    v = _heads(qkv[..., -GDN_VD:], GDN_DV)
    q = _l2norm(_heads(qkv[..., :GDN_KD], GDN_DK)) if with_q else None
    ab = col(OFF_A, OFF_ZB).astype(jnp.float32)
    ab = ab.reshape(ab.shape[:-1] + (2, 2, GDN_HEADS))
    g = -jnp.exp(a_log.astype(jnp.float32)) * jax.nn.softplus(ab[..., 0, :, :] + dt_bias.astype(jnp.float32))
    beta = jax.nn.sigmoid(ab[..., 1, :, :])
    return q, k, v, g, beta


def _chunks(t):
    t = jnp.moveaxis(t, 2, 1)
    t = t.reshape(t.shape[:2] + (-1, GDN_CHUNK) + t.shape[3:])
    return jnp.moveaxis(t, 2, 0)


def _unchunk(t):
    n, b, h, cl, d = t.shape
    return t.transpose(1, 0, 3, 2, 4).reshape(b, n * cl, h, d)


def _gdn_direction(q, k, v, beta, g, s0):
    kc, vc, bc, gc = (_chunks(t) for t in (k, v, beta, g))
    gcum = jnp.cumsum(gc, axis=-1)
    idx = jnp.arange(GDN_CHUNK)
    incl = idx[:, None] >= idx[None, :]
    decay = jnp.exp(jnp.where(incl, gcum[..., :, None] - gcum[..., None, :], -jnp.inf))
    kb = kc * bc[..., None]
    a_mat = jnp.where(idx[:, None] > idx[None, :],
                      jnp.einsum('nbhcd,nbhsd->nbhcs', kb, kc) * decay, 0.0)
    rhs = jnp.concatenate([vc * bc[..., None], kb * jnp.exp(gcum)[..., None]], axis=-1)
    sol = lax.linalg.triangular_solve(a_mat + jnp.eye(GDN_CHUNK, dtype=jnp.float32), rhs,
                                      left_side=True, lower=True)
    u_val, w_key = sol[..., :GDN_DV], sol[..., GDN_DV:]
    g_last = gcum[..., -1]
    k_tail = kc * jnp.exp(g_last[..., None] - gcum)[..., None]

    def advance(S, u_c, w_c, kt_c, gl_c):
        v_new = u_c - jnp.einsum('bhcd,bhde->bhce', w_c, S)
        S_new = S * jnp.exp(gl_c)[..., None, None] + jnp.einsum('bhcd,bhce->bhde', kt_c, v_new)
        return v_new, S_new

    if q is None:
        def state_step(S, xs):
            _, S_new = advance(S, *xs)
            return S_new, None
        s_final, _ = lax.scan(state_step, s0, (u_val, w_key, k_tail, g_last))
        return None, s_final

    qc = _chunks(q) * (GDN_DK ** -0.5)
    attn = jnp.einsum('nbhcd,nbhsd->nbhcs', qc, kc) * decay
    q_dec = qc * jnp.exp(gcum)[..., None]

    def out_step(S, xs):
        u_c, w_c, kt_c, gl_c, qd_c, at_c = xs
        v_new, S_new = advance(S, u_c, w_c, kt_c, gl_c)
        o = jnp.einsum('bhcd,bhde->bhce', qd_c, S) + jnp.einsum('bhcs,bhse->bhce', at_c, v_new)
        return S_new, o

    s_final, o = lax.scan(out_step, s0, (u_val, w_key, k_tail, g_last, q_dec, attn))
    return _unchunk(o), s_final


def _gdn_bidir(q, k, v, g, beta, s0_f, s0_b):
    rev = lambda t: None if t is None else jnp.flip(t, axis=1)
    o_f, s_f = _gdn_direction(q, k, v, beta[:, :, 0], g[:, :, 0], s0_f)
    o_b, s_b = _gdn_direction(rev(q), rev(k), rev(v), rev(beta[:, :, 1]), rev(g[:, :, 1]), s0_b)
    o = None if q is None else o_f + rev(o_b)
    return o, s_f, s_b


def _gmlp(pu, pv, pz, ln_g, ln_b, w_sp, b_sp, n_chunks):
    u = jax.nn.gelu(pu)
    v = _layernorm(jax.nn.gelu(pv), ln_g, ln_b)
    b, l, _ = v.shape
    v = v.reshape(b, n_chunks, GM_CHUNK, GM_GROUPS, GM_GROUP_DIM)
    s = jnp.einsum('gpq,bnqgc->bnpgc', w_sp, v) + jnp.transpose(b_sp)[:, :, None]
    return u * s.reshape(b, l, GM_WIDTH) * jax.nn.silu(pz)


def _stream_out(col, o_gdn, n_chunks, g_onorm, gm_ln_g, gm_ln_b, w_sp, b_sp, w_pa, w_pb, w_out):
    zb_raw = col(OFF_ZB, OFF_UA)
    z_b = _heads(zb_raw, GDN_DV).astype(jnp.float32)
    y_b = _rmsnorm(o_gdn, g_onorm) * jax.nn.silu(z_b)
    y_b = y_b.reshape(y_b.shape[:-2] + (GDN_VD,)).astype(zb_raw.dtype)
    y_a = _gmlp(col(OFF_UA, OFF_VA), col(OFF_VA, OFF_ZA), col(OFF_ZA, OFF_G),
                gm_ln_g, gm_ln_b, w_sp, b_sp, n_chunks)
    gates = jax.nn.sigmoid(col(OFF_G, IN_COLS))
    merged = gates[..., :D_MODEL] * (y_a @ w_pa) + gates[..., D_MODEL:] * (y_b @ w_pb)
    return merged @ w_out


def _hybrid_layer(x, ctx, mod_x, mod_c, n_chunks, update_ctx, g_pre, g_post, w_in, w_conv,
                  a_log, dt_bias, g_onorm, gm_ln_g, gm_ln_b, w_sp, b_sp, w_pa, w_pb, w_out):
    shift_x, scale_x, gate_x = mod_x
    shift_c, scale_c, gate_c = mod_c
    s_zero = jnp.zeros((ctx.shape[0], GDN_HEADS, GDN_DK, GDN_DV), jnp.float32)
    branch_w = (g_onorm, gm_ln_g, gm_ln_b, w_sp, b_sp, w_pa, w_pb, w_out)
    h_c = _rmsnorm(ctx, g_pre) * (1 + scale_c) + shift_c
    col_c = _projector(h_c, w_in, update_ctx)
    q_c, k_c, v_c, g_c, beta_c = _gdn_side(col_c, w_conv, a_log, dt_bias, update_ctx)
    o_c, s_fwd, s_bwd = _gdn_bidir(q_c, k_c, v_c, g_c, beta_c, s_zero, s_zero)
    h_x = _rmsnorm(x, g_pre) * (1 + scale_x) + shift_x
    col_x = _projector(h_x, w_in, True)
    q_x, k_x, v_x, g_x, beta_x = _gdn_side(col_x, w_conv, a_log, dt_bias, True)
    o_x, _, _ = _gdn_bidir(q_x, k_x, v_x, g_x, beta_x, s_fwd, s_bwd)
    x = x + gate_x * _rmsnorm(_stream_out(col_x, o_x, n_chunks, *branch_w), g_post)
    if update_ctx:
        ctx = ctx + gate_c * _rmsnorm(_stream_out(col_c, o_c, ctx.shape[1] // GM_CHUNK, *branch_w), g_post)
    return x, ctx


def setup_inputs(seed: int = 0) -> dict:
    key = jax.random.key(seed)
    ks = jax.random.split(key, 20)
    nrm = lambda k, shape, s: s * jax.random.normal(k, shape, jnp.float32)
    d = D_MODEL
    x = nrm(ks[0], (BATCH, SEQ, d), 1.0)
    c = nrm(ks[1], (BATCH, d), 1.0)
    ctx = nrm(ks[2], (BATCH, CTX_LEN, d), 1.0)
    c_ctx = nrm(ks[3], (d,), 1.0)
    w_mod = nrm(ks[4], (DEPTH, d, 3 * d), 0.5 * d ** -0.5)
    b_mod = nrm(ks[5], (DEPTH, 3 * d), 0.02)
    g_pre = 1.0 + nrm(ks[6], (DEPTH, d), 0.05)
    g_post = 1.0 + nrm(ks[7], (DEPTH, d), 0.05)
    w_in = nrm(ks[8], (DEPTH, d, IN_COLS), d ** -0.5)
    w_conv = nrm(ks[9], (DEPTH, GDN_CONV, OFF_A), GDN_CONV ** -0.5)
    a_log = jnp.log(jax.random.uniform(ks[10], (DEPTH, 2, GDN_HEADS), jnp.float32, 1.0, 16.0))
    dt = jnp.exp(jax.random.uniform(ks[11], (DEPTH, 2, GDN_HEADS), jnp.float32,
                                    float(np.log(1e-3)), float(np.log(1e-1))))
    dt_bias = dt + jnp.log(-jnp.expm1(-dt))
    g_onorm = 1.0 + nrm(ks[12], (DEPTH, GDN_DV), 0.05)
    gm_ln_g = 1.0 + nrm(ks[13], (DEPTH, GM_WIDTH), 0.05)
    gm_ln_b = nrm(ks[14], (DEPTH, GM_WIDTH), 0.02)
    w_sp = nrm(ks[15], (DEPTH, GM_GROUPS, GM_CHUNK, GM_CHUNK), GM_CHUNK ** -0.5)
    b_sp = 1.0 + nrm(ks[16], (DEPTH, GM_GROUPS, GM_CHUNK), 0.02)
    w_pa = nrm(ks[17], (DEPTH, GM_WIDTH, d), GM_WIDTH ** -0.5)
    w_pb = nrm(ks[18], (DEPTH, GDN_VD, d), GDN_VD ** -0.5)
    w_out = nrm(ks[19], (DEPTH, d, d), d ** -0.5)
    return {'x': x, 'c': c, 'ctx': ctx, 'c_ctx': c_ctx, 'w_mod': w_mod, 'b_mod': b_mod,
            'g_pre': g_pre, 'g_post': g_post, 'w_in': w_in, 'w_conv': w_conv, 'a_log': a_log,
            'dt_bias': dt_bias, 'g_onorm': g_onorm, 'gm_ln_g': gm_ln_g, 'gm_ln_b': gm_ln_b,
            'w_sp': w_sp, 'b_sp': b_sp, 'w_pa': w_pa, 'w_pb': w_pb, 'w_out': w_out}


def reference(x, c, ctx, c_ctx, w_mod, b_mod, g_pre, g_post, w_in, w_conv, a_log, dt_bias,
              g_onorm, gm_ln_g, gm_ln_b, w_sp, b_sp, w_pa, w_pb, w_out):
    rows = x.shape[1] // GRID_W
    n_chunks = rows // (GM_CHUNK // GRID_W)
    for i in range(DEPTH):
        mod_x = _modulation(c, w_mod[i], b_mod[i])
        mod_c = _modulation(c_ctx, w_mod[i], b_mod[i])
        x, ctx = _hybrid_layer(x, ctx, mod_x, mod_c, n_chunks, i + 1 < DEPTH, g_pre[i], g_post[i],
                               w_in[i], w_conv[i], a_log[i], dt_bias[i], g_onorm[i], gm_ln_g[i],
                               gm_ln_b[i], w_sp[i], b_sp[i], w_pa[i], w_pb[i], w_out[i])
    return x
```

```python
import functools

import jax
import jax.numpy as jnp
from jax import lax
from jax.experimental import pallas as pl
from jax.experimental.pallas import tpu as pltpu

F32 = jnp.float32
BF16 = jnp.bfloat16
HIGHEST = lax.Precision.HIGHEST

D_MODEL = 1024
HEADS = 8
HEAD_DIM = 128
GM_CHUNK = 128
EPS = 1e-6
NEG_BIG = -1e30

LANES = 128
BF16_SUBLANES = 16
GDN_CHUNK = 128
BASE_BLOCK = 8
VMEM_LIMIT = 56 * 1024 * 1024


def _sigmoid(x):
    return 1.0 / (1.0 + jnp.exp(-x))


def _silu(x):
    return x * _sigmoid(x)


def _gelu_tanh(x):
    c = 0.7978845608028654
    return 0.5 * x * (1.0 + jnp.tanh(c * (x + 0.044715 * (x * x * x))))


def _softplus(x):
    return jnp.maximum(x, 0.0) + jnp.log1p(jnp.exp(-jnp.abs(x)))


def _prenorm(x, g_pre, mod):
    ms = jnp.mean(x * x, axis=-1, keepdims=True)
    y = x * lax.rsqrt(ms + EPS) * g_pre
    return y * (1.0 + mod[:, D_MODEL:2 * D_MODEL]) + mod[:, 0:D_MODEL]


def _bdot(a, b):
    return jnp.dot(a.astype(BF16), b.astype(BF16), preferred_element_type=F32)


def _mod_kernel(c_ref, w_ref, b_ref, o_ref):
    s = _silu(c_ref[...])
    o_ref[...] = jnp.dot(s, w_ref[...], preferred_element_type=F32, precision=HIGHEST) + b_ref[...]


def _modulation(conds, w_mod, b_mod):
    n, d = conds.shape
    cols = w_mod.shape[1]
    tn = 512
    return pl.pallas_call(
        _mod_kernel,
        grid=(cols // tn,),
        in_specs=[pl.BlockSpec((n, d), lambda j: (0, 0)),
                  pl.BlockSpec((d, tn), lambda j: (0, j)),
                  pl.BlockSpec((1, tn), lambda j: (0, j))],
        out_specs=pl.BlockSpec((n, tn), lambda j: (0, j)),
        out_shape=jax.ShapeDtypeStruct((n, cols), F32),
        name="mod",
    )(conds, w_mod, b_mod.reshape(1, cols))


def _proj_kernel(x_ref, mod_ref, gpre_ref, w_ref, wab_ref, qkv_ref, ab_ref):
    h = _prenorm(x_ref[...], gpre_ref[...], mod_ref[...])
    hb = h.astype(BF16)
    for n in range(w_ref.shape[1] // D_MODEL):
        sl = slice(n * D_MODEL, (n + 1) * D_MODEL)
        qkv_ref[:, sl] = jnp.dot(hb, w_ref[:, sl], preferred_element_type=F32).astype(BF16)
    ab_ref[...] = jnp.dot(h, wab_ref[...], preferred_element_type=F32, precision=HIGHEST)


def _project_qkv(x2d, mod3, tiles_per_mod, g_pre, w_qkv, w_ab, tm):
    rows, d = x2d.shape
    nq = w_qkv.shape[1]
    return pl.pallas_call(
        _proj_kernel,
        grid=(rows // tm,),
        in_specs=[pl.BlockSpec((tm, d), lambda i: (i, 0)),
                  pl.BlockSpec((None, 1, 3 * d), lambda i: (i // tiles_per_mod, 0, 0)),
                  pl.BlockSpec((1, d), lambda i: (0, 0)),
                  pl.BlockSpec((d, nq), lambda i: (0, 0)),
                  pl.BlockSpec((d, LANES), lambda i: (0, 0))],
        out_specs=[pl.BlockSpec((tm, nq), lambda i: (i, 0)),
                   pl.BlockSpec((tm, LANES), lambda i: (i, 0))],
        out_shape=[jax.ShapeDtypeStruct((rows, nq), BF16),
                   jax.ShapeDtypeStruct((rows, LANES), F32)],
        compiler_params=pltpu.CompilerParams(vmem_limit_bytes=VMEM_LIMIT),
        name="proj",
    )(x2d, mod3, g_pre, w_qkv, w_ab)


def _unit_tri_inverse(a, eye, ii, jj):
    c = a.shape[0]
    shift = BASE_BLOCK.bit_length() - 1
    same = (ii >> shift) == (jj >> shift)
    d1 = jnp.where(same, a, 0.0)
    d2 = _bdot(d1, d1)
    z = _bdot(eye - d1, eye + d2)
    d4 = _bdot(d2, d2)
    t = z + _bdot(z, d4)
    b = BASE_BLOCK
    while b < c:
        s1 = b.bit_length() - 1
        off = ((ii >> (s1 + 1)) == (jj >> (s1 + 1))) & ((ii >> s1) != (jj >> s1))
        cl = jnp.where(off, a, 0.0)
        t = t - _bdot(_bdot(t, cl), t)
        b *= 2
    return t


def _gdn_kernel(*refs, reverse, with_q, accumulate, n_t, tg):
    it = iter(refs)
    qkv_ref, prev_ref, next_ref, ab_ref = next(it), next(it), next(it), next(it)
    wconv_ref, arow_ref, dtrow_ref, s0_ref = next(it), next(it), next(it), next(it)
    oin_ref = next(it) if accumulate else None
    o_ref = next(it) if with_q else None
    sfin_ref = None if with_q else next(it)
    y_scr, gc_scr, be_scr, s_scr = next(it), next(it), next(it), next(it)

    c = GDN_CHUNK
    i = pl.program_id(1)
    t = (n_t - 1 - i) if reverse else i
    ncol = qkv_ref.shape[1]
    koff = ncol - 2 * D_MODEL
    voff = ncol - D_MODEL

    @pl.when(i == 0)
    def _():
        s_scr[...] = s0_ref[...]

    p = qkv_ref[...].astype(F32)
    wc = wconv_ref[...]
    has_prev = (t > 0).astype(F32)
    has_next = (t < n_t - 1).astype(F32)
    prev_row = prev_ref[...].astype(F32)[BF16_SUBLANES - 1:BF16_SUBLANES, :] * has_prev
    next_row = next_ref[...].astype(F32)[0:1, :] * has_next
    rows = lax.broadcasted_iota(jnp.int32, (tg, 1), 0)
    p_dn = jnp.where(rows == 0, prev_row, pltpu.roll(p, 1, axis=0))
    p_up = jnp.where(rows == tg - 1, next_row, pltpu.roll(p, tg - 1, axis=0))
    y = p_dn * wc[0:1, :] + p * wc[1:2, :] + p_up * wc[2:3, :]
    y_scr[...] = _silu(y)

    ab = ab_ref[...]
    g_cmp = -jnp.exp(arow_ref[...]) * _softplus(ab + dtrow_ref[...])
    beta_cmp = _sigmoid(ab)
    ti = lax.broadcasted_iota(jnp.int32, (tg, tg), 0)
    tj = lax.broadcasted_iota(jnp.int32, (tg, tg), 1)
    cs = c.bit_length() - 1
    same_chunk = (ti >> cs) == (tj >> cs)
    order = (tj >= ti) if reverse else (tj <= ti)
    cum_mat = jnp.where(same_chunk & order, 1.0, 0.0).astype(F32)
    gcum_cmp = jnp.dot(cum_mat, g_cmp, preferred_element_type=F32, precision=HIGHEST)
    gcum_t = gcum_cmp.T
    el = lax.broadcasted_iota(jnp.int32, (LANES, D_MODEL), 0)
    eh = lax.broadcasted_iota(jnp.int32, (LANES, D_MODEL), 1) >> (HEAD_DIM.bit_length() - 1)
    dir_off = HEADS if reverse else 0
    sel_a = jnp.where(el == eh + dir_off, 1.0, 0.0).astype(F32)
    sel_b = jnp.where(el == eh + (2 * HEADS + dir_off), 1.0, 0.0).astype(F32)
    gc_scr[...] = jnp.dot(gcum_cmp, sel_a, preferred_element_type=F32, precision=HIGHEST)
    be_scr[...] = jnp.dot(beta_cmp, sel_b, preferred_element_type=F32, precision=HIGHEST)

    ii = lax.broadcasted_iota(jnp.int32, (c, c), 0)
    jj = lax.broadcasted_iota(jnp.int32, (c, c), 1)
    eye = jnp.where(ii == jj, 1.0, 0.0).astype(F32)
    incl = (ii <= jj) if reverse else (ii >= jj)
    strict = (ii < jj) if reverse else (ii > jj)
    q_scale = HEAD_DIM ** -0.5

    n_chunks = tg // c
    for cidx in (range(n_chunks - 1, -1, -1) if reverse else range(n_chunks)):
        r0 = cidx * c
        rs = slice(r0, r0 + c)
        last = r0 if reverse else r0 + c - 1
        for h in range(HEADS):
            hs = slice(h * HEAD_DIM, (h + 1) * HEAD_DIM)
            kr = y_scr[rs, koff + h * HEAD_DIM:koff + (h + 1) * HEAD_DIM]
            kn = kr * lax.rsqrt(jnp.sum(kr * kr, axis=-1, keepdims=True) + EPS)
            vv = y_scr[rs, voff + h * HEAD_DIM:voff + (h + 1) * HEAD_DIM]
            be = be_scr[rs, hs]
            gc = gc_scr[rs, hs]
            grow = gcum_t[dir_off + h:dir_off + h + 1, r0:r0 + c]
            glast = gc_scr[last:last + 1, hs]
            decay = jnp.exp(jnp.where(incl, gc - grow, NEG_BIG))
            kb = kn * be
            knb = kn.astype(BF16)
            if with_q:
                qr = y_scr[rs, hs]
                qs = qr * lax.rsqrt(jnp.sum(qr * qr, axis=-1, keepdims=True) + EPS) * q_scale
                lhs = jnp.concatenate([kb, qs], axis=0).astype(BF16)
            else:
                lhs = kb.astype(BF16)
            kk = lax.dot_general(lhs, knb, (((1,), (1,)), ((), ())), preferred_element_type=F32)
            a_mat = jnp.where(strict, kk[0:c] * decay, 0.0)
            t_inv = _unit_tri_inverse(a_mat, eye, ii, jj)
            eg = jnp.exp(gc)
            rhs = jnp.concatenate([vv * be, kb * eg], axis=1)
            sol = _bdot(t_inv, rhs)
            u_val = sol[:, 0:HEAD_DIM]
            w_key = sol[:, HEAD_DIM:2 * HEAD_DIM]
            s_old = s_scr[h]
            s_b = s_old.astype(BF16)
            if with_q:
                ws = jnp.dot(jnp.concatenate([w_key, qs * eg], axis=0).astype(BF16), s_b,
                             preferred_element_type=F32)
                v_new = u_val - ws[0:c]
                o = ws[c:2 * c] + _bdot(kk[c:2 * c] * decay, v_new)
                if accumulate:
                    o = o + oin_ref[rs, hs]
                o_ref[rs, hs] = o
            else:
                v_new = u_val - jnp.dot(w_key.astype(BF16), s_b, preferred_element_type=F32)
            k_tail = kn * jnp.exp(glast - gc)
            s_scr[h] = s_old * jnp.exp(glast) + _bdot(k_tail.T, v_new)

    if not with_q:
        @pl.when(i == n_t - 1)
        def _():
            sfin_ref[...] = s_scr[...]


def _gdn_scan(qkv, ab, w_conv, a_row, dt_row, s0, o_in, *, batch, reverse, with_q, tg):
    rows, ncol = qkv.shape
    seq = rows // batch
    n_t = seq // tg
    halo = BF16_SUBLANES
    hb = tg // halo
    last_halo = rows // halo - 1
    accumulate = o_in is not None

    def tile(b, i):
        return b * n_t + ((n_t - 1 - i) if reverse else i)

    in_specs = [
        pl.BlockSpec((tg, ncol), lambda b, i: (tile(b, i), 0)),
        pl.BlockSpec((halo, ncol), lambda b, i: (jnp.maximum(tile(b, i) * hb - 1, 0), 0)),
        pl.BlockSpec((halo, ncol), lambda b, i: (jnp.minimum((tile(b, i) + 1) * hb, last_halo), 0)),
        pl.BlockSpec((tg, LANES), lambda b, i: (tile(b, i), 0)),
        pl.BlockSpec((3, ncol), lambda b, i: (0, 0)),
        pl.BlockSpec((1, LANES), lambda b, i: (0, 0)),
        pl.BlockSpec((1, LANES), lambda b, i: (0, 0)),
        pl.BlockSpec((None, HEADS, HEAD_DIM, HEAD_DIM), lambda b, i: (b, 0, 0, 0)),
    ]
    args = [qkv, qkv, qkv, ab, w_conv, a_row, dt_row, s0]
    if accumulate:
        in_specs.append(pl.BlockSpec((tg, D_MODEL), lambda b, i: (tile(b, i), 0)))
        args.append(o_in)
    if with_q:
        out_specs = pl.BlockSpec((tg, D_MODEL), lambda b, i: (tile(b, i), 0))
        out_shape = jax.ShapeDtypeStruct((rows, D_MODEL), F32)
    else:
        out_specs = pl.BlockSpec((None, HEADS, HEAD_DIM, HEAD_DIM), lambda b, i: (b, 0, 0, 0))
        out_shape = jax.ShapeDtypeStruct((batch, HEADS, HEAD_DIM, HEAD_DIM), F32)
    kern = functools.partial(_gdn_kernel, reverse=reverse, with_q=with_q, accumulate=accumulate,
                             n_t=n_t, tg=tg)
    return pl.pallas_call(
        kern,
        grid=(batch, n_t),
        in_specs=in_specs,
        out_specs=out_specs,
        out_shape=out_shape,
        scratch_shapes=[pltpu.VMEM((tg, ncol), F32),
                        pltpu.VMEM((tg, D_MODEL), F32),
                        pltpu.VMEM((tg, D_MODEL), F32),
                        pltpu.VMEM((HEADS, HEAD_DIM, HEAD_DIM), F32)],
        compiler_params=pltpu.CompilerParams(dimension_semantics=("arbitrary", "arbitrary"),
                                             vmem_limit_bytes=VMEM_LIMIT),
        name="gdn_bwd" if reverse else "gdn_fwd",
    )(*args)


def _brancha_kernel(x_ref, mod_ref, gpre_ref, wa_ref, lng_ref, lnb_ref, wsp_ref, bsp_ref, wpa_ref, ma_ref):
    d = D_MODEL
    tm = x_ref.shape[0]
    hb = _prenorm(x_ref[...], gpre_ref[...], mod_ref[...]).astype(BF16)
    vg = _gelu_tanh(jnp.dot(hb, wa_ref[:, d:2 * d], preferred_element_type=F32))
    mu = jnp.mean(vg, axis=-1, keepdims=True)
    vc = vg - mu
    var = jnp.mean(vc * vc, axis=-1, keepdims=True)
    v = (vc * lax.rsqrt(var + EPS) * lng_ref[...] + lnb_ref[...]).astype(BF16)
    parts = []
    for n in range(tm // GM_CHUNK):
        rs = slice(n * GM_CHUNK, (n + 1) * GM_CHUNK)
        cols = [jnp.dot(wsp_ref[g], v[rs, g * HEAD_DIM:(g + 1) * HEAD_DIM], preferred_element_type=F32)
                for g in range(d // HEAD_DIM)]
        parts.append(jnp.concatenate(cols, axis=1) + bsp_ref[...])
    s = jnp.concatenate(parts, axis=0)
    u = _gelu_tanh(jnp.dot(hb, wa_ref[:, 0:d], preferred_element_type=F32))
    z = _silu(jnp.dot(hb, wa_ref[:, 2 * d:3 * d], preferred_element_type=F32))
    ya = (u * s * z).astype(BF16)
    gate = _sigmoid(jnp.dot(hb, wa_ref[:, 3 * d:4 * d], preferred_element_type=F32))
    ma_ref[...] = gate * jnp.dot(ya, wpa_ref[...], preferred_element_type=F32)


def _branch_a(x2d, mod3, tiles_per_mod, g_pre, w_a, ln_g, ln_b, w_sp, b_sp_e, w_pa, tm):
    rows, d = x2d.shape
    const = lambda i: (0, 0)
    return pl.pallas_call(
        _brancha_kernel,
        grid=(rows // tm,),
        in_specs=[pl.BlockSpec((tm, d), lambda i: (i, 0)),
                  pl.BlockSpec((None, 1, 3 * d), lambda i: (i // tiles_per_mod, 0, 0)),
                  pl.BlockSpec((1, d), const),
                  pl.BlockSpec((d, 4 * d), const),
                  pl.BlockSpec((1, d), const),
                  pl.BlockSpec((1, d), const),
                  pl.BlockSpec(w_sp.shape, lambda i: (0, 0, 0)),
                  pl.BlockSpec((GM_CHUNK, d), const),
                  pl.BlockSpec((d, d), const)],
        out_specs=pl.BlockSpec((tm, d), lambda i: (i, 0)),
        out_shape=jax.ShapeDtypeStruct((rows, d), F32),
        compiler_params=pltpu.CompilerParams(vmem_limit_bytes=VMEM_LIMIT),
        name="brancha",
    )(x2d, mod3, g_pre, w_a, ln_g, ln_b, w_sp, b_sp_e, w_pa)


def _out_kernel(x_ref, mod_ref, gpre_ref, gpost_ref, o_ref, ma_ref, wzb_ref, wgb_ref, wpb_ref, wout_ref,
                gon_ref, out_ref):
    d = D_MODEL
    x = x_ref[...]
    mod = mod_ref[...]
    hb = _prenorm(x, gpre_ref[...], mod).astype(BF16)
    zb = _silu(jnp.dot(hb, wzb_ref[...], preferred_element_type=F32))
    gon = gon_ref[...]
    parts = []
    for h in range(HEADS):
        oh = o_ref[:, h * HEAD_DIM:(h + 1) * HEAD_DIM]
        ms = jnp.mean(oh * oh, axis=-1, keepdims=True)
        parts.append(oh * lax.rsqrt(ms + EPS) * gon)
    yb = (jnp.concatenate(parts, axis=1) * zb).astype(BF16)
    gate_b = _sigmoid(jnp.dot(hb, wgb_ref[...], preferred_element_type=F32))
    merged = ma_ref[...] + gate_b * jnp.dot(yb, wpb_ref[...], preferred_element_type=F32)
    z = jnp.dot(merged.astype(BF16), wout_ref[...], preferred_element_type=F32)
    zn = z * lax.rsqrt(jnp.mean(z * z, axis=-1, keepdims=True) + EPS) * gpost_ref[...]
    out_ref[...] = x + mod[:, 2 * d:3 * d] * zn


def _merge_out(x2d, mod3, tiles_per_mod, g_pre, g_post, o, m_a, w_zb, w_gb, w_pb, w_out, g_onorm, tm):
    rows, d = x2d.shape
    const = lambda i: (0, 0)
    row = lambda i: (i, 0)
    return pl.pallas_call(
        _out_kernel,
        grid=(rows // tm,),
        in_specs=[pl.BlockSpec((tm, d), row),
                  pl.BlockSpec((None, 1, 3 * d), lambda i: (i // tiles_per_mod, 0, 0)),
                  pl.BlockSpec((1, d), const),
                  pl.BlockSpec((1, d), const),
                  pl.BlockSpec((tm, d), row),
                  pl.BlockSpec((tm, d), row),
                  pl.BlockSpec((d, d), const),
                  pl.BlockSpec((d, d), const),
                  pl.BlockSpec((d, d), const),
                  pl.BlockSpec((d, d), const),
                  pl.BlockSpec((1, HEAD_DIM), const)],
        out_specs=pl.BlockSpec((tm, d), row),
        out_shape=jax.ShapeDtypeStruct((rows, d), F32),
        compiler_params=pltpu.CompilerParams(vmem_limit_bytes=VMEM_LIMIT),
        name="merge_out",
    )(x2d, mod3, g_pre, g_post, o, m_a, w_zb, w_gb, w_pb, w_out, g_onorm)


def _layer(x, ctx, mod_x, mod_c, g_pre, g_post, w_in, w_conv, a_log, dt_bias, g_onorm, gm_ln_g, gm_ln_b,
           w_sp, b_sp, w_pa, w_pb, w_out):
    batch, seq, d = x.shape
    ctx_len = ctx.shape[1]
    kd = HEADS * HEAD_DIM
    off_a = 3 * kd
    off_zb = off_a + 4 * HEADS
    off_ua = off_zb + kd
    off_g = off_ua + 3 * d

    w_qkv = w_in[:, 0:off_a].astype(BF16)
    w_kv = w_in[:, kd:off_a].astype(BF16)
    w_ab = jnp.pad(w_in[:, off_a:off_zb], ((0, 0), (0, LANES - 4 * HEADS)))
    w_zb = w_in[:, off_zb:off_ua].astype(BF16)
    w_a = w_in[:, off_ua:off_g + d].astype(BF16)
    w_gb = w_in[:, off_g + d:off_g + 2 * d].astype(BF16)
    a_row = jnp.pad(a_log.reshape(1, 2 * HEADS), ((0, 0), (0, LANES - 2 * HEADS)))
    dt_row = jnp.pad(dt_bias.reshape(1, 2 * HEADS), ((0, 0), (0, LANES - 2 * HEADS)))
    b_sp_e = jnp.repeat(b_sp.T, HEAD_DIM, axis=1)
    g_pre2 = g_pre.reshape(1, d)

    x2d = x.reshape(batch * seq, d)
    c2d = ctx.reshape(batch * ctx_len, d)
    tm = 512
    tpm_x = seq // tm

    kv_c, ab_c = _project_qkv(c2d, mod_c, 1 << 30, g_pre2, w_kv, w_ab, ctx_len)
    s_zero = jnp.zeros((batch, HEADS, HEAD_DIM, HEAD_DIM), F32)
    wconv_kv = w_conv[:, kd:]
    s_fwd = _gdn_scan(kv_c, ab_c, wconv_kv, a_row, dt_row, s_zero, None,
                      batch=batch, reverse=False, with_q=False, tg=ctx_len)
    s_bwd = _gdn_scan(kv_c, ab_c, wconv_kv, a_row, dt_row, s_zero, None,
                      batch=batch, reverse=True, with_q=False, tg=ctx_len)

    qkv, ab = _project_qkv(x2d, mod_x, tpm_x, g_pre2, w_qkv, w_ab, tm)
    tg = 2 * GDN_CHUNK
    o_f = _gdn_scan(qkv, ab, w_conv, a_row, dt_row, s_fwd, None,
                    batch=batch, reverse=False, with_q=True, tg=tg)
    o = _gdn_scan(qkv, ab, w_conv, a_row, dt_row, s_bwd, o_f,
                  batch=batch, reverse=True, with_q=True, tg=tg)
    m_a = _branch_a(x2d, mod_x, tpm_x, g_pre2, w_a, gm_ln_g.reshape(1, d), gm_ln_b.reshape(1, d),
                    w_sp.astype(BF16), b_sp_e, w_pa.astype(BF16), tm)
    out = _merge_out(x2d, mod_x, tpm_x, g_pre2, g_post.reshape(1, d), o, m_a, w_zb, w_gb,
                     w_pb.astype(BF16), w_out.astype(BF16), g_onorm.reshape(1, HEAD_DIM), tm)
    return out.reshape(batch, seq, d)


def kernel(x, c, ctx, c_ctx, w_mod, b_mod, g_pre, g_post, w_in, w_conv, a_log, dt_bias, g_onorm, gm_ln_g,
           gm_ln_b, w_sp, b_sp, w_pa, w_pb, w_out):
    batch, _, d = x.shape
    depth = w_mod.shape[0]
    assert depth == 1, "context tokens are only read (never updated) by a single-layer stack"
    assert ctx.shape[1] % GDN_CHUNK == 0 and x.shape[1] % (4 * GDN_CHUNK) == 0 and d == D_MODEL
    pad = (-(batch + 1)) % 8
    conds = jnp.concatenate([c, c_ctx[None, :], jnp.zeros((pad, d), F32)], axis=0)
    i = 0
    m = _modulation(conds, w_mod[i], b_mod[i])
    mod_x = m[0:batch].reshape(batch, 1, 3 * d)
    mod_c = m[batch:batch + 1].reshape(1, 1, 3 * d)
    return _layer(x, ctx, mod_x, mod_c, g_pre[i], g_post[i], w_in[i], w_conv[i], a_log[i], dt_bias[i],
                  g_onorm[i], gm_ln_g[i], gm_ln_b[i], w_sp[i], b_sp[i], w_pa[i], w_pb[i], w_out[i])
```

```python
import functools

import jax
import jax.numpy as jnp
from jax import lax
from jax.experimental import pallas as pl
from jax.experimental.pallas import tpu as pltpu

F32 = jnp.float32
BF16 = jnp.bfloat16
HIGHEST = lax.Precision.HIGHEST

D_MODEL = 1024
HEADS = 8
HEAD_DIM = 128
GM_CHUNK = 128
EPS = 1e-6
NEG_BIG = -1e30

LANES = 128
F32_SUBLANES = 8
GDN_CHUNK = 128
BASE_BLOCK = 8
VMEM_LIMIT = 56 * 1024 * 1024


def _sigmoid(x):
    return 1.0 / (1.0 + jnp.exp(-x))


def _silu(x):
    return x * _sigmoid(x)


def _gelu_tanh(x):
    c = 0.7978845608028654
    return 0.5 * x * (1.0 + jnp.tanh(c * (x + 0.044715 * (x * x * x))))


def _softplus(x):
    return jnp.maximum(x, 0.0) + jnp.log1p(jnp.exp(-jnp.abs(x)))


def _prenorm(x, g_pre, mod):
    ms = jnp.mean(x * x, axis=-1, keepdims=True)
    y = x * lax.rsqrt(ms + EPS) * g_pre
    return y * (1.0 + mod[:, D_MODEL:2 * D_MODEL]) + mod[:, 0:D_MODEL]


def _bdot(a, b):
    return jnp.dot(a.astype(BF16), b.astype(BF16), preferred_element_type=F32)


def _mod_kernel(c_ref, w_ref, b_ref, o_ref):
    s = _silu(c_ref[...])
    o_ref[...] = jnp.dot(s, w_ref[...], preferred_element_type=F32, precision=HIGHEST) + b_ref[...]


def _modulation(conds, w_mod, b_mod):
    n, d = conds.shape
    cols = w_mod.shape[1]
    tn = 512
    return pl.pallas_call(
        _mod_kernel,
        grid=(cols // tn,),
        in_specs=[pl.BlockSpec((n, d), lambda j: (0, 0)),
                  pl.BlockSpec((d, tn), lambda j: (0, j)),
                  pl.BlockSpec((1, tn), lambda j: (0, j))],
        out_specs=pl.BlockSpec((n, tn), lambda j: (0, j)),
        out_shape=jax.ShapeDtypeStruct((n, cols), F32),
        name="mod",
    )(conds, w_mod, b_mod.reshape(1, cols))


def _proj_kernel(x_ref, xp_ref, xn_ref, mod_ref, gpre_ref, w_ref, wab_ref, wconv_ref, y_ref, ab_ref, *,
                 kinds, tiles_per_seq):
    tm = x_ref.shape[0]
    halo = xp_ref.shape[0]
    pos = lax.rem(pl.program_id(0), tiles_per_seq)
    xa = jnp.concatenate([xp_ref[...], x_ref[...], xn_ref[...]], axis=0)
    h = _prenorm(xa, gpre_ref[...], mod_ref[...])
    hb = h.astype(BF16)
    rows = lax.broadcasted_iota(jnp.int32, (tm, 1), 0)
    keep_dn = (rows != 0) | (pos != 0)
    keep_up = (rows != tm - 1) | (pos != tiles_per_seq - 1)
    for n, kind in enumerate(kinds):
        sl = slice(n * D_MODEL, (n + 1) * D_MODEL)
        p = jnp.dot(hb, w_ref[:, sl], preferred_element_type=F32)
        wc = wconv_ref[:, sl]
        p_dn = jnp.where(keep_dn, pltpu.roll(p, 1, axis=0)[halo:halo + tm], 0.0)
        p_up = jnp.where(keep_up, pltpu.roll(p, tm + 2 * halo - 1, axis=0)[halo:halo + tm], 0.0)
        y = _silu(p_dn * wc[0:1, :] + p[halo:halo + tm] * wc[1:2, :] + p_up * wc[2:3, :])
        if kind == "v":
            y_ref[:, sl] = y.astype(BF16)
            continue
        scale = HEAD_DIM ** -0.5 if kind == "q" else 1.0
        for hd in range(HEADS):
            yh = y[:, hd * HEAD_DIM:(hd + 1) * HEAD_DIM]
            inv = lax.rsqrt(jnp.sum(yh * yh, axis=-1, keepdims=True) + EPS) * scale
            y_ref[:, n * D_MODEL + hd * HEAD_DIM:n * D_MODEL + (hd + 1) * HEAD_DIM] = (yh * inv).astype(BF16)
    ab_ref[...] = jnp.dot(h[halo:halo + tm], wab_ref[...], preferred_element_type=F32, precision=HIGHEST)


def _project_qkv(x2d, mod3, tiles_per_mod, g_pre, w_qkv, w_ab, w_conv, tm, seq, kinds):
    rows, d = x2d.shape
    nq = w_qkv.shape[1]
    halo = F32_SUBLANES
    hb = tm // halo
    last_halo = rows // halo - 1
    kern = functools.partial(_proj_kernel, kinds=kinds, tiles_per_seq=seq // tm)
    return pl.pallas_call(
        kern,
        grid=(rows // tm,),
        in_specs=[pl.BlockSpec((tm, d), lambda i: (i, 0)),
                  pl.BlockSpec((halo, d), lambda i: (jnp.maximum(i * hb - 1, 0), 0)),
                  pl.BlockSpec((halo, d), lambda i: (jnp.minimum((i + 1) * hb, last_halo), 0)),
                  pl.BlockSpec((None, 1, 3 * d), lambda i: (i // tiles_per_mod, 0, 0)),
                  pl.BlockSpec((1, d), lambda i: (0, 0)),
                  pl.BlockSpec((d, nq), lambda i: (0, 0)),
                  pl.BlockSpec((d, LANES), lambda i: (0, 0)),
                  pl.BlockSpec((3, nq), lambda i: (0, 0))],
        out_specs=[pl.BlockSpec((tm, nq), lambda i: (i, 0)),
                   pl.BlockSpec((tm, LANES), lambda i: (i, 0))],
        out_shape=[jax.ShapeDtypeStruct((rows, nq), BF16),
                   jax.ShapeDtypeStruct((rows, LANES), F32)],
        compiler_params=pltpu.CompilerParams(vmem_limit_bytes=VMEM_LIMIT),
        name="proj",
    )(x2d, x2d, x2d, mod3, g_pre, w_qkv, w_ab, w_conv)


def _unit_tri_inverse(a_list, eye, ii, jj):
    c = a_list[0].shape[0]
    shift = BASE_BLOCK.bit_length() - 1
    same = (ii >> shift) == (jj >> shift)
    d1 = [jnp.where(same, a, 0.0).astype(BF16) for a in a_list]
    d2 = [jnp.dot(d, d, preferred_element_type=F32) for d in d1]
    d2b = [d.astype(BF16) for d in d2]
    z = [jnp.dot((eye - x.astype(F32)).astype(BF16), (eye + y).astype(BF16), preferred_element_type=F32)
         for x, y in zip(d1, d2)]
    d4 = [jnp.dot(d, d, preferred_element_type=F32) for d in d2b]
    t = [x + _bdot(x, y) for x, y in zip(z, d4)]
    b = BASE_BLOCK
    while b < c:
        s1 = b.bit_length() - 1
        off = ((ii >> (s1 + 1)) == (jj >> (s1 + 1))) & ((ii >> s1) != (jj >> s1))
        tb = [x.astype(BF16) for x in t]
        tc = [jnp.dot(x, jnp.where(off, a, 0.0).astype(BF16), preferred_element_type=F32)
              for x, a in zip(tb, a_list)]
        t = [x - jnp.dot(y.astype(BF16), xb, preferred_element_type=F32) for x, y, xb in zip(t, tc, tb)]
        b *= 2
    return t


def _gdn_kernel(*refs, reverse, with_q, accumulate, n_t, tg):
    it = iter(refs)
    y_ref, ab_ref, arow_ref, dtrow_ref, s0_ref = next(it), next(it), next(it), next(it), next(it)
    oin_ref = next(it) if accumulate else None
    o_ref = next(it) if with_q else None
    sfin_ref = None if with_q else next(it)
    s_scr = next(it)

    c = GDN_CHUNK
    i = pl.program_id(1)
    ncol = y_ref.shape[1]
    koff = ncol - 2 * D_MODEL
    voff = ncol - D_MODEL

    @pl.when(i == 0)
    def _():
        s_scr[...] = s0_ref[...]

    ab = ab_ref[...]
    g_cmp = -jnp.exp(arow_ref[...]) * _softplus(ab + dtrow_ref[...])
    beta_cmp = _sigmoid(ab)
    ti = lax.broadcasted_iota(jnp.int32, (tg, tg), 0)
    tj = lax.broadcasted_iota(jnp.int32, (tg, tg), 1)
    cs = c.bit_length() - 1
    same_chunk = (ti >> cs) == (tj >> cs)
    order = (tj >= ti) if reverse else (tj <= ti)
    cum_mat = jnp.where(same_chunk & order, 1.0, 0.0).astype(F32)
    gcum_cmp = jnp.dot(cum_mat, g_cmp, preferred_element_type=F32, precision=HIGHEST)
    gcum_t = gcum_cmp.T
    eg_cmp = jnp.exp(gcum_cmp)
    dir_off = HEADS if reverse else 0

    ii = lax.broadcasted_iota(jnp.int32, (c, c), 0)
    jj = lax.broadcasted_iota(jnp.int32, (c, c), 1)
    eye = jnp.where(ii == jj, 1.0, 0.0).astype(F32)
    incl = (ii <= jj) if reverse else (ii >= jj)
    strict = (ii < jj) if reverse else (ii > jj)

    n_chunks = tg // c
    chunk_order = list(range(n_chunks - 1, -1, -1) if reverse else range(n_chunks))
    inst = [(cidx, h) for cidx in chunk_order for h in range(HEADS)]

    knb, qsb, be, gc, eg, decay = [], [], [], [], [], []
    for cidx, h in inst:
        rs = slice(cidx * c, (cidx + 1) * c)
        knb.append(y_ref[rs, koff + h * HEAD_DIM:koff + (h + 1) * HEAD_DIM])
        if with_q:
            qsb.append(y_ref[rs, h * HEAD_DIM:(h + 1) * HEAD_DIM])
        la = dir_off + h
        lb = 2 * HEADS + dir_off + h
        be.append(jnp.broadcast_to(beta_cmp[rs, lb:lb + 1], (c, HEAD_DIM)))
        g_col = jnp.broadcast_to(gcum_cmp[rs, la:la + 1], (c, HEAD_DIM))
        g_row = gcum_t[la:la + 1, cidx * c:(cidx + 1) * c]
        gc.append(g_col)
        decay.append(jnp.exp(jnp.where(incl, g_col - g_row, NEG_BIG)))
        eg.append(jnp.broadcast_to(eg_cmp[rs, la:la + 1], (c, HEAD_DIM)))
    kn = [k.astype(F32) for k in knb]
    kb = [k * b for k, b in zip(kn, be)]
    nt_dims = (((1,), (1,)), ((), ()))
    if with_q:
        qs = [q.astype(F32) for q in qsb]
        kk = [lax.dot_general(jnp.concatenate([a.astype(BF16), q], axis=0), k, nt_dims,
                              preferred_element_type=F32) for a, q, k in zip(kb, qsb, knb)]
        attn = [(x[c:2 * c] * d).astype(BF16) for x, d in zip(kk, decay)]
    else:
        kk = [lax.dot_general(a.astype(BF16), k, nt_dims, preferred_element_type=F32)
              for a, k in zip(kb, knb)]
    a_mat = [jnp.where(strict, x[0:c] * d, 0.0) for x, d in zip(kk, decay)]
    t_inv = _unit_tri_inverse(a_mat, eye, ii, jj)
    sol = []
    for n, (cidx, h) in enumerate(inst):
        rs = slice(cidx * c, (cidx + 1) * c)
        vv = y_ref[rs, voff + h * HEAD_DIM:voff + (h + 1) * HEAD_DIM].astype(F32)
        rhs = jnp.concatenate([vv * be[n], kb[n] * eg[n]], axis=1)
        sol.append(_bdot(t_inv[n], rhs))

    for ci in range(n_chunks):
        base = ci * HEADS
        idx = range(base, base + HEADS)
        rs = slice(chunk_order[ci] * c, (chunk_order[ci] + 1) * c)
        s_old = [s_scr[h] for h in range(HEADS)]
        s_b = [s.astype(BF16) for s in s_old]
        if with_q:
            ws = [jnp.dot(jnp.concatenate([sol[n][:, HEAD_DIM:], qs[n] * eg[n]], axis=0).astype(BF16), sb,
                          preferred_element_type=F32) for n, sb in zip(idx, s_b)]
        else:
            ws = [jnp.dot(sol[n][:, HEAD_DIM:].astype(BF16), sb, preferred_element_type=F32)
                  for n, sb in zip(idx, s_b)]
        v_new = [(sol[n][:, 0:HEAD_DIM] - w[0:c]).astype(BF16) for n, w in zip(idx, ws)]
        last = rs.start if reverse else rs.stop - 1
        gl_cmp = gcum_cmp[last:last + 1, :]
        tail_cmp = jnp.exp(gl_cmp - gcum_cmp[rs, :])
        egl_cmp = jnp.exp(gl_cmp)
        k_tail = [(kn[n] * jnp.broadcast_to(tail_cmp[:, dir_off + h:dir_off + h + 1], (c, HEAD_DIM))).T
                  .astype(BF16) for h, n in enumerate(idx)]
        kv = [jnp.dot(kt, v, preferred_element_type=F32) for kt, v in zip(k_tail, v_new)]
        for h in range(HEADS):
            s_decay = jnp.broadcast_to(egl_cmp[:, dir_off + h:dir_off + h + 1], (HEAD_DIM, HEAD_DIM))
            s_scr[h] = s_old[h] * s_decay + kv[h]
        if with_q:
            av = [jnp.dot(attn[n], v, preferred_element_type=F32) for n, v in zip(idx, v_new)]
            for h in range(HEADS):
                hs = slice(h * HEAD_DIM, (h + 1) * HEAD_DIM)
                o = ws[h][c:2 * c] + av[h]
                if accumulate:
                    o = o + oin_ref[rs, hs]
                o_ref[rs, hs] = o

    if not with_q:
        @pl.when(i == n_t - 1)
        def _():
            sfin_ref[...] = s_scr[...]


def _gdn_scan(y, ab, a_row, dt_row, s0, o_in, *, batch, reverse, with_q, tg):
    rows, ncol = y.shape
    n_t = rows // batch // tg
    accumulate = o_in is not None

    def tile(b, i):
        return (b * n_t + ((n_t - 1 - i) if reverse else i), 0)

    state = lambda b, i: (b, 0, 0, 0)
    const = lambda b, i: (0, 0)
    in_specs = [
        pl.BlockSpec((tg, ncol), tile),
        pl.BlockSpec((tg, LANES), tile),
        pl.BlockSpec((1, LANES), const),
        pl.BlockSpec((1, LANES), const),
        pl.BlockSpec((None, HEADS, HEAD_DIM, HEAD_DIM), state),
    ]
    args = [y, ab, a_row, dt_row, s0]
    if accumulate:
        in_specs.append(pl.BlockSpec((tg, D_MODEL), tile))
        args.append(o_in)
    if with_q:
        out_specs = pl.BlockSpec((tg, D_MODEL), tile)
        out_shape = jax.ShapeDtypeStruct((rows, D_MODEL), F32)
    else:
        out_specs = pl.BlockSpec((None, HEADS, HEAD_DIM, HEAD_DIM), state)
        out_shape = jax.ShapeDtypeStruct((batch, HEADS, HEAD_DIM, HEAD_DIM), F32)
    kern = functools.partial(_gdn_kernel, reverse=reverse, with_q=with_q, accumulate=accumulate,
                             n_t=n_t, tg=tg)
    return pl.pallas_call(
        kern,
        grid=(batch, n_t),
        in_specs=in_specs,
        out_specs=out_specs,
        out_shape=out_shape,
        scratch_shapes=[pltpu.VMEM((HEADS, HEAD_DIM, HEAD_DIM), F32)],
        compiler_params=pltpu.CompilerParams(dimension_semantics=("arbitrary", "arbitrary"),
                                             vmem_limit_bytes=VMEM_LIMIT),
        name="gdn_bwd" if reverse else "gdn_fwd",
    )(*args)


def _brancha_kernel(x_ref, mod_ref, gpre_ref, wa_ref, lng_ref, lnb_ref, wsp_ref, bsp_ref, wpa_ref, ma_ref):
    d = D_MODEL
    tm = x_ref.shape[0]
    hb = _prenorm(x_ref[...], gpre_ref[...], mod_ref[...]).astype(BF16)
    vg = _gelu_tanh(jnp.dot(hb, wa_ref[:, d:2 * d], preferred_element_type=F32))
    mu = jnp.mean(vg, axis=-1, keepdims=True)
    vc = vg - mu
    var = jnp.mean(vc * vc, axis=-1, keepdims=True)
    v = (vc * lax.rsqrt(var + EPS) * lng_ref[...] + lnb_ref[...]).astype(BF16)
    parts = []
    for n in range(tm // GM_CHUNK):
        rs = slice(n * GM_CHUNK, (n + 1) * GM_CHUNK)
        cols = [jnp.dot(wsp_ref[g], v[rs, g * HEAD_DIM:(g + 1) * HEAD_DIM], preferred_element_type=F32)
                for g in range(d // HEAD_DIM)]
        parts.append(jnp.concatenate(cols, axis=1) + bsp_ref[...])
    s = jnp.concatenate(parts, axis=0)
    u = _gelu_tanh(jnp.dot(hb, wa_ref[:, 0:d], preferred_element_type=F32))
    z = _silu(jnp.dot(hb, wa_ref[:, 2 * d:3 * d], preferred_element_type=F32))
    ya = (u * s * z).astype(BF16)
    gate = _sigmoid(jnp.dot(hb, wa_ref[:, 3 * d:4 * d], preferred_element_type=F32))
    ma_ref[...] = gate * jnp.dot(ya, wpa_ref[...], preferred_element_type=F32)


def _branch_a(x2d, mod3, tiles_per_mod, g_pre, w_a, ln_g, ln_b, w_sp, b_sp_e, w_pa, tm):
    rows, d = x2d.shape
    const = lambda i: (0, 0)
    return pl.pallas_call(
        _brancha_kernel,
        grid=(rows // tm,),
        in_specs=[pl.BlockSpec((tm, d), lambda i: (i, 0)),
                  pl.BlockSpec((None, 1, 3 * d), lambda i: (i // tiles_per_mod, 0, 0)),
                  pl.BlockSpec((1, d), const),
                  pl.BlockSpec((d, 4 * d), const),
                  pl.BlockSpec((1, d), const),
                  pl.BlockSpec((1, d), const),
                  pl.BlockSpec(w_sp.shape, lambda i: (0, 0, 0)),
                  pl.BlockSpec((GM_CHUNK, d), const),
                  pl.BlockSpec((d, d), const)],
        out_specs=pl.BlockSpec((tm, d), lambda i: (i, 0)),
        out_shape=jax.ShapeDtypeStruct((rows, d), F32),
        compiler_params=pltpu.CompilerParams(vmem_limit_bytes=VMEM_LIMIT),
        name="brancha",
    )(x2d, mod3, g_pre, w_a, ln_g, ln_b, w_sp, b_sp_e, w_pa)


def _out_kernel(x_ref, mod_ref, gpre_ref, gpost_ref, o_ref, ma_ref, wzb_ref, wgb_ref, wpb_ref, wout_ref,
                gon_ref, out_ref):
    d = D_MODEL
    x = x_ref[...]
    mod = mod_ref[...]
    hb = _prenorm(x, gpre_ref[...], mod).astype(BF16)
    zb = _silu(jnp.dot(hb, wzb_ref[...], preferred_element_type=F32))
    gon = gon_ref[...]
    parts = []
    for h in range(HEADS):
        oh = o_ref[:, h * HEAD_DIM:(h + 1) * HEAD_DIM]
        ms = jnp.mean(oh * oh, axis=-1, keepdims=True)
        parts.append(oh * lax.rsqrt(ms + EPS) * gon)
    yb = (jnp.concatenate(parts, axis=1) * zb).astype(BF16)
    gate_b = _sigmoid(jnp.dot(hb, wgb_ref[...], preferred_element_type=F32))
    merged = ma_ref[...] + gate_b * jnp.dot(yb, wpb_ref[...], preferred_element_type=F32)
    z = jnp.dot(merged.astype(BF16), wout_ref[...], preferred_element_type=F32)
    zn = z * lax.rsqrt(jnp.mean(z * z, axis=-1, keepdims=True) + EPS) * gpost_ref[...]
    out_ref[...] = x + mod[:, 2 * d:3 * d] * zn


def _merge_out(x2d, mod3, tiles_per_mod, g_pre, g_post, o, m_a, w_zb, w_gb, w_pb, w_out, g_onorm, tm):
    rows, d = x2d.shape
    const = lambda i: (0, 0)
    row = lambda i: (i, 0)
    return pl.pallas_call(
        _out_kernel,
        grid=(rows // tm,),
        in_specs=[pl.BlockSpec((tm, d), row),
                  pl.BlockSpec((None, 1, 3 * d), lambda i: (i // tiles_per_mod, 0, 0)),
                  pl.BlockSpec((1, d), const),
                  pl.BlockSpec((1, d), const),
                  pl.BlockSpec((tm, d), row),
                  pl.BlockSpec((tm, d), row),
                  pl.BlockSpec((d, d), const),
                  pl.BlockSpec((d, d), const),
                  pl.BlockSpec((d, d), const),
                  pl.BlockSpec((d, d), const),
                  pl.BlockSpec((1, HEAD_DIM), const)],
        out_specs=pl.BlockSpec((tm, d), row),
        out_shape=jax.ShapeDtypeStruct((rows, d), F32),
        compiler_params=pltpu.CompilerParams(vmem_limit_bytes=VMEM_LIMIT),
        name="merge_out",
    )(x2d, mod3, g_pre, g_post, o, m_a, w_zb, w_gb, w_pb, w_out, g_onorm)


def _layer(x, ctx, mod_x, mod_c, g_pre, g_post, w_in, w_conv, a_log, dt_bias, g_onorm, gm_ln_g, gm_ln_b,
           w_sp, b_sp, w_pa, w_pb, w_out):
    batch, seq, d = x.shape
    ctx_len = ctx.shape[1]
    kd = HEADS * HEAD_DIM
    off_a = 3 * kd
    off_zb = off_a + 4 * HEADS
    off_ua = off_zb + kd
    off_g = off_ua + 3 * d

    w_qkv = w_in[:, 0:off_a].astype(BF16)
    w_kv = w_in[:, kd:off_a].astype(BF16)
    w_ab = jnp.pad(w_in[:, off_a:off_zb], ((0, 0), (0, LANES - 4 * HEADS)))
    w_zb = w_in[:, off_zb:off_ua].astype(BF16)
    w_a = w_in[:, off_ua:off_g + d].astype(BF16)
    w_gb = w_in[:, off_g + d:off_g + 2 * d].astype(BF16)
    a_row = jnp.pad(a_log.reshape(1, 2 * HEADS), ((0, 0), (0, LANES - 2 * HEADS)))
    dt_row = jnp.pad(dt_bias.reshape(1, 2 * HEADS), ((0, 0), (0, LANES - 2 * HEADS)))
    b_sp_e = jnp.repeat(b_sp.T, HEAD_DIM, axis=1)
    g_pre2 = g_pre.reshape(1, d)

    x2d = x.reshape(batch * seq, d)
    c2d = ctx.reshape(batch * ctx_len, d)
    tm = 512
    tpm_x = seq // tm

    kv_c, ab_c = _project_qkv(c2d, mod_c, 1 << 30, g_pre2, w_kv, w_ab, w_conv[:, kd:], ctx_len, ctx_len,
                              ("k", "v"))
    s_zero = jnp.zeros((batch, HEADS, HEAD_DIM, HEAD_DIM), F32)
    s_fwd = _gdn_scan(kv_c, ab_c, a_row, dt_row, s_zero, None,
                      batch=batch, reverse=False, with_q=False, tg=ctx_len)
    s_bwd = _gdn_scan(kv_c, ab_c, a_row, dt_row, s_zero, None,
                      batch=batch, reverse=True, with_q=False, tg=ctx_len)

    qkv, ab = _project_qkv(x2d, mod_x, tpm_x, g_pre2, w_qkv, w_ab, w_conv, tm, seq, ("q", "k", "v"))
    tg = 2 * GDN_CHUNK
    o_f = _gdn_scan(qkv, ab, a_row, dt_row, s_fwd, None, batch=batch, reverse=False, with_q=True, tg=tg)
    o = _gdn_scan(qkv, ab, a_row, dt_row, s_bwd, o_f, batch=batch, reverse=True, with_q=True, tg=tg)
    m_a = _branch_a(x2d, mod_x, tpm_x, g_pre2, w_a, gm_ln_g.reshape(1, d), gm_ln_b.reshape(1, d),
                    w_sp.astype(BF16), b_sp_e, w_pa.astype(BF16), tm)
    out = _merge_out(x2d, mod_x, tpm_x, g_pre2, g_post.reshape(1, d), o, m_a, w_zb, w_gb,
                     w_pb.astype(BF16), w_out.astype(BF16), g_onorm.reshape(1, HEAD_DIM), tm)
    return out.reshape(batch, seq, d)


def kernel(x, c, ctx, c_ctx, w_mod, b_mod, g_pre, g_post, w_in, w_conv, a_log, dt_bias, g_onorm, gm_ln_g,
           gm_ln_b, w_sp, b_sp, w_pa, w_pb, w_out):
    batch, _, d = x.shape
    depth = w_mod.shape[0]
    assert depth == 1, "context tokens are only read (never updated) by a single-layer stack"
    assert ctx.shape[1] % GDN_CHUNK == 0 and x.shape[1] % (4 * GDN_CHUNK) == 0 and d == D_MODEL
    pad = (-(batch + 1)) % 8
    conds = jnp.concatenate([c, c_ctx[None, :], jnp.zeros((pad, d), F32)], axis=0)
    i = 0
    m = _modulation(conds, w_mod[i], b_mod[i])
    mod_x = m[0:batch].reshape(batch, 1, 3 * d)
    mod_c = m[batch:batch + 1].reshape(1, 1, 3 * d)
    return _layer(x, ctx, mod_x, mod_c, g_pre[i], g_post[i], w_in[i], w_conv[i], a_log[i], dt_bias[i],
                  g_onorm[i], gm_ln_g[i], gm_ln_b[i], w_sp[i], b_sp[i], w_pa[i], w_pb[i], w_out[i])
```

```python
import functools

import jax
import jax.numpy as jnp
from jax import lax
from jax.experimental import pallas as pl
from jax.experimental.pallas import tpu as pltpu

F32 = jnp.float32
BF16 = jnp.bfloat16
HIGHEST = lax.Precision.HIGHEST

D_MODEL = 1024
HEADS = 8
HEAD_DIM = 128
GM_CHUNK = 128
EPS = 1e-6
NEG_BIG = -1e30

LANES = 128
F32_SUBLANES = 8
GDN_CHUNK = 128
BASE_BLOCK = 8
VMEM_LIMIT = 56 * 1024 * 1024


def _sigmoid(x):
    return 1.0 / (1.0 + jnp.exp(-x))


def _silu(x):
    return x * _sigmoid(x)


def _gelu_tanh(x):
    c = 0.7978845608028654
    return 0.5 * x * (1.0 + jnp.tanh(c * (x + 0.044715 * (x * x * x))))


def _softplus(x):
    return jnp.maximum(x, 0.0) + jnp.log1p(jnp.exp(-jnp.abs(x)))


def _prenorm(x, g_pre, mod):
    ms = jnp.mean(x * x, axis=-1, keepdims=True)
    y = x * lax.rsqrt(ms + EPS) * g_pre
    return y * (1.0 + mod[:, D_MODEL:2 * D_MODEL]) + mod[:, 0:D_MODEL]


def _bdot(a, b):
    return jnp.dot(a.astype(BF16), b.astype(BF16), preferred_element_type=F32)


def _split_bf16(x, parts):
    out, r = [], x
    for _ in range(parts):
        p = r.astype(BF16)
        out.append(p)
        r = r - p.astype(F32)
    return out


def _mod_kernel(c_ref, w_ref, b_ref, o_ref):
    s = _silu(c_ref[...])
    o_ref[...] = jnp.dot(s, w_ref[...], preferred_element_type=F32, precision=HIGHEST) + b_ref[...]


def _modulation(conds, w_mod, b_mod):
    n, d = conds.shape
    cols = w_mod.shape[1]
    tn = 512
    return pl.pallas_call(
        _mod_kernel,
        grid=(cols // tn,),
        in_specs=[pl.BlockSpec((n, d), lambda j: (0, 0)),
                  pl.BlockSpec((d, tn), lambda j: (0, j)),
                  pl.BlockSpec((1, tn), lambda j: (0, j))],
        out_specs=pl.BlockSpec((n, tn), lambda j: (0, j)),
        out_shape=jax.ShapeDtypeStruct((n, cols), F32),
        name="mod",
    )(conds, w_mod, b_mod.reshape(1, cols))


def _proj_kernel(x_ref, xp_ref, xn_ref, mod_ref, gpre_ref, w_ref, wab_ref, wconv_ref, y_ref, ab_ref, *,
                 kinds, col0, tiles_per_seq):
    tm = x_ref.shape[0]
    halo = xp_ref.shape[0]
    pos = lax.rem(pl.program_id(0), tiles_per_seq)
    xa = jnp.concatenate([xp_ref[...], x_ref[...], xn_ref[...]], axis=0)
    h = _prenorm(xa, gpre_ref[...], mod_ref[...])
    hb = h.astype(BF16)
    rows = lax.broadcasted_iota(jnp.int32, (tm, 1), 0)
    keep_dn = (rows != 0) | (pos != 0)
    keep_up = (rows != tm - 1) | (pos != tiles_per_seq - 1)
    for n, kind in enumerate(kinds):
        sl = slice(n * D_MODEL, (n + 1) * D_MODEL)
        wsl = slice(col0 + n * D_MODEL, col0 + (n + 1) * D_MODEL)
        p = jnp.dot(hb, w_ref[:, wsl], preferred_element_type=F32)
        wc = wconv_ref[:, wsl]
        p_dn = jnp.where(keep_dn, pltpu.roll(p, 1, axis=0)[halo:halo + tm], 0.0)
        p_up = jnp.where(keep_up, pltpu.roll(p, tm + 2 * halo - 1, axis=0)[halo:halo + tm], 0.0)
        y = _silu(p_dn * wc[0:1, :] + p[halo:halo + tm] * wc[1:2, :] + p_up * wc[2:3, :])
        if kind == "v":
            y_ref[:, sl] = y.astype(BF16)
            continue
        scale = HEAD_DIM ** -0.5 if kind == "q" else 1.0
        for hd in range(HEADS):
            yh = y[:, hd * HEAD_DIM:(hd + 1) * HEAD_DIM]
            inv = lax.rsqrt(jnp.sum(yh * yh, axis=-1, keepdims=True) + EPS) * scale
            y_ref[:, n * D_MODEL + hd * HEAD_DIM:n * D_MODEL + (hd + 1) * HEAD_DIM] = (yh * inv).astype(BF16)
    h_hi, h_lo = _split_bf16(h[halo:halo + tm], 2)
    hw = jnp.dot(h_hi, wab_ref[...], preferred_element_type=F32)
    ab_ref[...] = (hw[:, 0:LANES] + hw[:, LANES:2 * LANES]
                   + jnp.dot(h_lo, wab_ref[:, 0:LANES], preferred_element_type=F32))


def _project_qkv(x2d, mod3, tiles_per_mod, g_pre, w_pack, w_ab, w_conv, tm, seq, kinds):
    rows, d = x2d.shape
    nq = len(kinds) * d
    nw = 3 * d
    halo = F32_SUBLANES
    hb = tm // halo
    last_halo = rows // halo - 1
    kern = functools.partial(_proj_kernel, kinds=kinds, col0=nw - nq, tiles_per_seq=seq // tm)
    return pl.pallas_call(
        kern,
        grid=(rows // tm,),
        in_specs=[pl.BlockSpec((tm, d), lambda i: (i, 0)),
                  pl.BlockSpec((halo, d), lambda i: (jnp.maximum(i * hb - 1, 0), 0)),
                  pl.BlockSpec((halo, d), lambda i: (jnp.minimum((i + 1) * hb, last_halo), 0)),
                  pl.BlockSpec((None, 1, 3 * d), lambda i: (i // tiles_per_mod, 0, 0)),
                  pl.BlockSpec((1, d), lambda i: (0, 0)),
                  pl.BlockSpec((d, nw), lambda i: (0, 0)),
                  pl.BlockSpec((d, 2 * LANES), lambda i: (0, 0)),
                  pl.BlockSpec((3, nw), lambda i: (0, 0))],
        out_specs=[pl.BlockSpec((tm, nq), lambda i: (i, 0)),
                   pl.BlockSpec((tm, LANES), lambda i: (i, 0))],
        out_shape=[jax.ShapeDtypeStruct((rows, nq), BF16),
                   jax.ShapeDtypeStruct((rows, LANES), F32)],
        compiler_params=pltpu.CompilerParams(vmem_limit_bytes=VMEM_LIMIT),
        name="proj",
    )(x2d, x2d, x2d, mod3, g_pre, w_pack, w_ab, w_conv)


def _unit_tri_inverse(a_list, eye, ii, jj):
    c = a_list[0].shape[0]
    shift = BASE_BLOCK.bit_length() - 1
    same = (ii >> shift) == (jj >> shift)
    d1 = [jnp.where(same, a, 0.0).astype(BF16) for a in a_list]
    d2 = [jnp.dot(d, d, preferred_element_type=F32) for d in d1]
    d2b = [d.astype(BF16) for d in d2]
    z = [jnp.dot((eye - x.astype(F32)).astype(BF16), (eye + y).astype(BF16), preferred_element_type=F32)
         for x, y in zip(d1, d2)]
    d4 = [jnp.dot(d, d, preferred_element_type=F32) for d in d2b]
    t = [x + _bdot(x, y) for x, y in zip(z, d4)]
    b = BASE_BLOCK
    while b < c:
        s1 = b.bit_length() - 1
        off = ((ii >> (s1 + 1)) == (jj >> (s1 + 1))) & ((ii >> s1) != (jj >> s1))
        tb = [x.astype(BF16) for x in t]
        tc = [jnp.dot(x, jnp.where(off, a, 0.0).astype(BF16), preferred_element_type=F32)
              for x, a in zip(tb, a_list)]
        t = [x - jnp.dot(y.astype(BF16), xb, preferred_element_type=F32) for x, y, xb in zip(t, tc, tb)]
        b *= 2
    return t


def _gdn_kernel(*refs, reverse, with_q, accumulate, n_t, tg):
    it = iter(refs)
    y_ref, ab_ref, arow_ref, dtrow_ref, s0_ref = next(it), next(it), next(it), next(it), next(it)
    oin_ref = next(it) if accumulate else None
    o_ref = next(it) if with_q else None
    sfin_ref = None if with_q else next(it)
    s_scr = next(it)

    c = GDN_CHUNK
    i = pl.program_id(1)
    ncol = y_ref.shape[1]
    koff = ncol - 2 * D_MODEL
    voff = ncol - D_MODEL

    @pl.when(i == 0)
    def _():
        s_scr[...] = s0_ref[...]

    ab = ab_ref[...]
    g_cmp = -jnp.exp(arow_ref[...]) * _softplus(ab + dtrow_ref[...])
    beta_cmp = _sigmoid(ab)
    ti = lax.broadcasted_iota(jnp.int32, (tg, tg), 0)
    tj = lax.broadcasted_iota(jnp.int32, (tg, tg), 1)
    cs = c.bit_length() - 1
    same_chunk = (ti >> cs) == (tj >> cs)
    order = (tj >= ti) if reverse else (tj <= ti)
    cum_mat = jnp.where(same_chunk & order, 1.0, 0.0).astype(BF16)
    gcum3 = jnp.dot(cum_mat, jnp.concatenate(_split_bf16(g_cmp, 3), axis=1), preferred_element_type=F32)
    gcum_cmp = gcum3[:, 0:LANES] + gcum3[:, LANES:2 * LANES] + gcum3[:, 2 * LANES:3 * LANES]
    gcum_t = gcum_cmp.T
    eg_cmp = jnp.exp(gcum_cmp)
    dir_off = HEADS if reverse else 0

    ii = lax.broadcasted_iota(jnp.int32, (c, c), 0)
    jj = lax.broadcasted_iota(jnp.int32, (c, c), 1)
    eye = jnp.where(ii == jj, 1.0, 0.0).astype(F32)
    incl = (ii <= jj) if reverse else (ii >= jj)
    strict = (ii < jj) if reverse else (ii > jj)

    n_chunks = tg // c
    chunk_order = list(range(n_chunks - 1, -1, -1) if reverse else range(n_chunks))
    inst = [(cidx, h) for cidx in chunk_order for h in range(HEADS)]

    knb, qsb, be, gc, eg, decay = [], [], [], [], [], []
    for cidx, h in inst:
        rs = slice(cidx * c, (cidx + 1) * c)
        knb.append(y_ref[rs, koff + h * HEAD_DIM:koff + (h + 1) * HEAD_DIM])
        if with_q:
            qsb.append(y_ref[rs, h * HEAD_DIM:(h + 1) * HEAD_DIM])
        la = dir_off + h
        lb = 2 * HEADS + dir_off + h
        be.append(jnp.broadcast_to(beta_cmp[rs, lb:lb + 1], (c, HEAD_DIM)))
        g_col = jnp.broadcast_to(gcum_cmp[rs, la:la + 1], (c, HEAD_DIM))
        g_row = gcum_t[la:la + 1, cidx * c:(cidx + 1) * c]
        gc.append(g_col)
        decay.append(jnp.exp(jnp.where(incl, g_col - g_row, NEG_BIG)))
        eg.append(jnp.broadcast_to(eg_cmp[rs, la:la + 1], (c, HEAD_DIM)))
    kn = [k.astype(F32) for k in knb]
    kb = [k * b for k, b in zip(kn, be)]
    nt_dims = (((1,), (1,)), ((), ()))
    if with_q:
        qs = [q.astype(F32) for q in qsb]
        kk = [lax.dot_general(jnp.concatenate([a.astype(BF16), q], axis=0), k, nt_dims,
                              preferred_element_type=F32) for a, q, k in zip(kb, qsb, knb)]
        attn = [(x[c:2 * c] * d).astype(BF16) for x, d in zip(kk, decay)]
    else:
        kk = [lax.dot_general(a.astype(BF16), k, nt_dims, preferred_element_type=F32)
              for a, k in zip(kb, knb)]
    a_mat = [jnp.where(strict, x[0:c] * d, 0.0) for x, d in zip(kk, decay)]
    t_inv = _unit_tri_inverse(a_mat, eye, ii, jj)
    sol = []
    for n, (cidx, h) in enumerate(inst):
        rs = slice(cidx * c, (cidx + 1) * c)
        vv = y_ref[rs, voff + h * HEAD_DIM:voff + (h + 1) * HEAD_DIM].astype(F32)
        rhs = jnp.concatenate([vv * be[n], kb[n] * eg[n]], axis=1)
        sol.append(_bdot(t_inv[n], rhs))

    for ci in range(n_chunks):
        base = ci * HEADS
        idx = range(base, base + HEADS)
        rs = slice(chunk_order[ci] * c, (chunk_order[ci] + 1) * c)
        s_old = [s_scr[h] for h in range(HEADS)]
        s_b = [s.astype(BF16) for s in s_old]
        if with_q:
            ws = [jnp.dot(jnp.concatenate([sol[n][:, HEAD_DIM:], qs[n] * eg[n]], axis=0).astype(BF16), sb,
                          preferred_element_type=F32) for n, sb in zip(idx, s_b)]
        else:
            ws = [jnp.dot(sol[n][:, HEAD_DIM:].astype(BF16), sb, preferred_element_type=F32)
                  for n, sb in zip(idx, s_b)]
        v_new = [(sol[n][:, 0:HEAD_DIM] - w[0:c]).astype(BF16) for n, w in zip(idx, ws)]
        last = rs.start if reverse else rs.stop - 1
        gl_cmp = gcum_cmp[last:last + 1, :]
        tail_cmp = jnp.exp(gl_cmp - gcum_cmp[rs, :])
        egl_cmp = jnp.exp(gl_cmp)
        k_tail = [(kn[n] * jnp.broadcast_to(tail_cmp[:, dir_off + h:dir_off + h + 1], (c, HEAD_DIM))).T
                  .astype(BF16) for h, n in enumerate(idx)]
        kv = [jnp.dot(kt, v, preferred_element_type=F32) for kt, v in zip(k_tail, v_new)]
        for h in range(HEADS):
            s_decay = jnp.broadcast_to(egl_cmp[:, dir_off + h:dir_off + h + 1], (HEAD_DIM, HEAD_DIM))
            s_scr[h] = s_old[h] * s_decay + kv[h]
        if with_q:
            av = [jnp.dot(attn[n], v, preferred_element_type=F32) for n, v in zip(idx, v_new)]
            for h in range(HEADS):
                hs = slice(h * HEAD_DIM, (h + 1) * HEAD_DIM)
                o = ws[h][c:2 * c] + av[h]
                if accumulate:
                    o = o + oin_ref[rs, hs]
                o_ref[rs, hs] = o

    if not with_q:
        @pl.when(i == n_t - 1)
        def _():
            sfin_ref[...] = s_scr[...]


def _gdn_scan(y, ab, a_row, dt_row, s0, o_in, *, batch, reverse, with_q, tg):
    rows, ncol = y.shape
    n_t = rows // batch // tg
    accumulate = o_in is not None

    def tile(b, i):
        return (b * n_t + ((n_t - 1 - i) if reverse else i), 0)

    state = lambda b, i: (b, 0, 0, 0)
    const = lambda b, i: (0, 0)
    in_specs = [
        pl.BlockSpec((tg, ncol), tile),
        pl.BlockSpec((tg, LANES), tile),
        pl.BlockSpec((1, LANES), const),
        pl.BlockSpec((1, LANES), const),
        pl.BlockSpec((None, HEADS, HEAD_DIM, HEAD_DIM), state),
    ]
    args = [y, ab, a_row, dt_row, s0]
    if accumulate:
        in_specs.append(pl.BlockSpec((tg, D_MODEL), tile))
        args.append(o_in)
    if with_q:
        out_specs = pl.BlockSpec((tg, D_MODEL), tile)
        out_shape = jax.ShapeDtypeStruct((rows, D_MODEL), F32)
    else:
        out_specs = pl.BlockSpec((None, HEADS, HEAD_DIM, HEAD_DIM), state)
        out_shape = jax.ShapeDtypeStruct((batch, HEADS, HEAD_DIM, HEAD_DIM), F32)
    kern = functools.partial(_gdn_kernel, reverse=reverse, with_q=with_q, accumulate=accumulate,
                             n_t=n_t, tg=tg)
    return pl.pallas_call(
        kern,
        grid=(batch, n_t),
        in_specs=in_specs,
        out_specs=out_specs,
        out_shape=out_shape,
        scratch_shapes=[pltpu.VMEM((HEADS, HEAD_DIM, HEAD_DIM), F32)],
        compiler_params=pltpu.CompilerParams(dimension_semantics=("arbitrary", "arbitrary"),
                                             vmem_limit_bytes=VMEM_LIMIT),
        name="gdn_bwd" if reverse else "gdn_fwd",
    )(*args)


def _brancha_kernel(x_ref, mod_ref, gpre_ref, wa_ref, lng_ref, lnb_ref, wsp_ref, bsp_ref, wpa_ref, ma_ref):
    d = D_MODEL
    tm = x_ref.shape[0]
    hb = _prenorm(x_ref[...], gpre_ref[...], mod_ref[...]).astype(BF16)
    vg = _gelu_tanh(jnp.dot(hb, wa_ref[:, d:2 * d], preferred_element_type=F32))
    mu = jnp.mean(vg, axis=-1, keepdims=True)
    vc = vg - mu
    var = jnp.mean(vc * vc, axis=-1, keepdims=True)
    v = (vc * lax.rsqrt(var + EPS) * lng_ref[...] + lnb_ref[...]).astype(BF16)
    parts = []
    for n in range(tm // GM_CHUNK):
        rs = slice(n * GM_CHUNK, (n + 1) * GM_CHUNK)
        cols = [jnp.dot(wsp_ref[g], v[rs, g * HEAD_DIM:(g + 1) * HEAD_DIM], preferred_element_type=F32)
                for g in range(d // HEAD_DIM)]
        parts.append(jnp.concatenate(cols, axis=1) + bsp_ref[...])
    s = jnp.concatenate(parts, axis=0)
    u = _gelu_tanh(jnp.dot(hb, wa_ref[:, 0:d], preferred_element_type=F32))
    z = _silu(jnp.dot(hb, wa_ref[:, 2 * d:3 * d], preferred_element_type=F32))
    ya = (u * s * z).astype(BF16)
    gate = _sigmoid(jnp.dot(hb, wa_ref[:, 3 * d:4 * d], preferred_element_type=F32))
    ma_ref[...] = gate * jnp.dot(ya, wpa_ref[...], preferred_element_type=F32)


def _branch_a(x2d, mod3, tiles_per_mod, g_pre, w_pack, ln_g, ln_b, w_sp, b_sp_e, w_pa, tm):
    rows, d = x2d.shape
    const = lambda i: (0, 0)
    return pl.pallas_call(
        _brancha_kernel,
        grid=(rows // tm,),
        in_specs=[pl.BlockSpec((tm, d), lambda i: (i, 0)),
                  pl.BlockSpec((None, 1, 3 * d), lambda i: (i // tiles_per_mod, 0, 0)),
                  pl.BlockSpec((1, d), const),
                  pl.BlockSpec((d, 4 * d), lambda i: (0, 1)),
                  pl.BlockSpec((1, d), const),
                  pl.BlockSpec((1, d), const),
                  pl.BlockSpec(w_sp.shape, lambda i: (0, 0, 0)),
                  pl.BlockSpec((GM_CHUNK, d), const),
                  pl.BlockSpec((d, d), const)],
        out_specs=pl.BlockSpec((tm, d), lambda i: (i, 0)),
        out_shape=jax.ShapeDtypeStruct((rows, d), F32),
        compiler_params=pltpu.CompilerParams(vmem_limit_bytes=VMEM_LIMIT),
        name="brancha",
    )(x2d, mod3, g_pre, w_pack, ln_g, ln_b, w_sp, b_sp_e, w_pa)


def _out_kernel(x_ref, mod_ref, gpre_ref, gpost_ref, o_ref, ma_ref, wzb_ref, wgb_ref, wpb_ref, wout_ref,
                gon_ref, out_ref):
    d = D_MODEL
    x = x_ref[...]
    mod = mod_ref[...]
    hb = _prenorm(x, gpre_ref[...], mod).astype(BF16)
    zb = _silu(jnp.dot(hb, wzb_ref[...], preferred_element_type=F32))
    gon = gon_ref[...]
    parts = []
    for h in range(HEADS):
        oh = o_ref[:, h * HEAD_DIM:(h + 1) * HEAD_DIM]
        ms = jnp.mean(oh * oh, axis=-1, keepdims=True)
        parts.append(oh * lax.rsqrt(ms + EPS) * gon)
    yb = (jnp.concatenate(parts, axis=1) * zb).astype(BF16)
    gate_b = _sigmoid(jnp.dot(hb, wgb_ref[...], preferred_element_type=F32))
    merged = ma_ref[...] + gate_b * jnp.dot(yb, wpb_ref[...], preferred_element_type=F32)
    z = jnp.dot(merged.astype(BF16), wout_ref[...], preferred_element_type=F32)
    zn = z * lax.rsqrt(jnp.mean(z * z, axis=-1, keepdims=True) + EPS) * gpost_ref[...]
    out_ref[...] = x + mod[:, 2 * d:3 * d] * zn


def _merge_out(x2d, mod3, tiles_per_mod, g_pre, g_post, o, m_a, w_pack, w_pb, w_out, g_onorm, tm):
    rows, d = x2d.shape
    const = lambda i: (0, 0)
    row = lambda i: (i, 0)
    return pl.pallas_call(
        _out_kernel,
        grid=(rows // tm,),
        in_specs=[pl.BlockSpec((tm, d), row),
                  pl.BlockSpec((None, 1, 3 * d), lambda i: (i // tiles_per_mod, 0, 0)),
                  pl.BlockSpec((1, d), const),
                  pl.BlockSpec((1, d), const),
                  pl.BlockSpec((tm, d), row),
                  pl.BlockSpec((tm, d), row),
                  pl.BlockSpec((d, d), lambda i: (0, 3)),
                  pl.BlockSpec((d, d), lambda i: (0, 8)),
                  pl.BlockSpec((d, d), const),
                  pl.BlockSpec((d, d), const),
                  pl.BlockSpec((1, HEAD_DIM), const)],
        out_specs=pl.BlockSpec((tm, d), row),
        out_shape=jax.ShapeDtypeStruct((rows, d), F32),
        compiler_params=pltpu.CompilerParams(vmem_limit_bytes=VMEM_LIMIT),
        name="merge_out",
    )(x2d, mod3, g_pre, g_post, o, m_a, w_pack, w_pack, w_pb, w_out, g_onorm)


def _layer(x, ctx, mod_x, mod_c, g_pre, g_post, w_in, w_conv, a_log, dt_bias, g_onorm, gm_ln_g, gm_ln_b,
           w_sp, b_sp, w_pa, w_pb, w_out):
    batch, seq, d = x.shape
    ctx_len = ctx.shape[1]
    kd = HEADS * HEAD_DIM
    off_a = 3 * kd
    off_zb = off_a + 4 * HEADS

    w_pack = jnp.concatenate([w_in[:, 0:off_a], w_in[:, off_zb:]], axis=1).astype(BF16)
    w_ab = jnp.pad(w_in[:, off_a:off_zb], ((0, 0), (0, LANES - 4 * HEADS)))
    w_ab2 = jnp.concatenate(_split_bf16(w_ab, 2), axis=1)
    a_row = jnp.pad(a_log.reshape(1, 2 * HEADS), ((0, 0), (0, LANES - 2 * HEADS)))
    dt_row = jnp.pad(dt_bias.reshape(1, 2 * HEADS), ((0, 0), (0, LANES - 2 * HEADS)))
    b_sp_e = jnp.repeat(b_sp.T, HEAD_DIM, axis=1)
    g_pre2 = g_pre.reshape(1, d)

    x2d = x.reshape(batch * seq, d)
    c2d = ctx.reshape(batch * ctx_len, d)
    tm = 512
    tm2 = 2 * tm

    kv_c, ab_c = _project_qkv(c2d, mod_c, 1 << 30, g_pre2, w_pack, w_ab2, w_conv, ctx_len, ctx_len, ("k", "v"))
    s_zero = jnp.zeros((batch, HEADS, HEAD_DIM, HEAD_DIM), F32)
    s_fwd = _gdn_scan(kv_c, ab_c, a_row, dt_row, s_zero, None,
                      batch=batch, reverse=False, with_q=False, tg=ctx_len)
    s_bwd = _gdn_scan(kv_c, ab_c, a_row, dt_row, s_zero, None,
                      batch=batch, reverse=True, with_q=False, tg=ctx_len)

    qkv, ab = _project_qkv(x2d, mod_x, seq // tm, g_pre2, w_pack, w_ab2, w_conv, tm, seq, ("q", "k", "v"))
    tg = 2 * GDN_CHUNK
    o_f = _gdn_scan(qkv, ab, a_row, dt_row, s_fwd, None, batch=batch, reverse=False, with_q=True, tg=tg)
    o = _gdn_scan(qkv, ab, a_row, dt_row, s_bwd, o_f, batch=batch, reverse=True, with_q=True, tg=tg)
    m_a = _branch_a(x2d, mod_x, seq // tm2, g_pre2, w_pack, gm_ln_g.reshape(1, d), gm_ln_b.reshape(1, d),
                    w_sp.astype(BF16), b_sp_e, w_pa.astype(BF16), tm2)
    out = _merge_out(x2d, mod_x, seq // tm2, g_pre2, g_post.reshape(1, d), o, m_a, w_pack,
                     w_pb.astype(BF16), w_out.astype(BF16), g_onorm.reshape(1, HEAD_DIM), tm2)
    return out.reshape(batch, seq, d)


def kernel(x, c, ctx, c_ctx, w_mod, b_mod, g_pre, g_post, w_in, w_conv, a_log, dt_bias, g_onorm, gm_ln_g,
           gm_ln_b, w_sp, b_sp, w_pa, w_pb, w_out):
    batch, _, d = x.shape
    depth = w_mod.shape[0]
    assert depth == 1, "context tokens are only read (never updated) by a single-layer stack"
    assert ctx.shape[1] % GDN_CHUNK == 0 and x.shape[1] % (8 * GDN_CHUNK) == 0 and d == D_MODEL
    pad = (-(batch + 1)) % 8
    conds = jnp.concatenate([c, c_ctx[None, :], jnp.zeros((pad, d), F32)], axis=0)
    i = 0
    m = _modulation(conds, w_mod[i], b_mod[i])
    mod_x = m[0:batch].reshape(batch, 1, 3 * d)
    mod_c = m[batch:batch + 1].reshape(1, 1, 3 * d)
    return _layer(x, ctx, mod_x, mod_c, g_pre[i], g_post[i], w_in[i], w_conv[i], a_log[i], dt_bias[i],
                  g_onorm[i], gm_ln_g[i], gm_ln_b[i], w_sp[i], b_sp[i], w_pa[i], w_pb[i], w_out[i])
```

```python
import functools

import jax
import jax.numpy as jnp
from jax import lax
from jax.experimental import pallas as pl
from jax.experimental.pallas import tpu as pltpu

F32 = jnp.float32
BF16 = jnp.bfloat16
HIGHEST = lax.Precision.HIGHEST

D_MODEL = 1024
HEADS = 8
HEAD_DIM = 128
GM_CHUNK = 128
EPS = 1e-6
NEG_BIG = -1e30

LANES = 128
F32_SUBLANES = 8
GDN_CHUNK = 128
BASE_BLOCK = 8
VMEM_LIMIT = 56 * 1024 * 1024


def _sigmoid(x):
    return 1.0 / (1.0 + jnp.exp(-x))


def _silu(x):
    return x * _sigmoid(x)


def _gelu_tanh(x):
    c = 0.7978845608028654
    return 0.5 * x * (1.0 + jnp.tanh(c * (x + 0.044715 * (x * x * x))))


def _softplus(x):
    return jnp.maximum(x, 0.0) + jnp.log1p(jnp.exp(-jnp.abs(x)))


def _prenorm(x, g_pre, mod):
    ms = jnp.mean(x * x, axis=-1, keepdims=True)
    y = x * lax.rsqrt(ms + EPS) * g_pre
    return y * (1.0 + mod[:, D_MODEL:2 * D_MODEL]) + mod[:, 0:D_MODEL]


def _bdot(a, b):
    return jnp.dot(a.astype(BF16), b.astype(BF16), preferred_element_type=F32)


def _split_bf16(x, parts):
    out, r = [], x
    for _ in range(parts):
        p = r.astype(BF16)
        out.append(p)
        r = r - p.astype(F32)
    return out


def _mod_kernel(c_ref, w_ref, b_ref, o_ref):
    s = _silu(c_ref[...])
    o_ref[...] = jnp.dot(s, w_ref[...], preferred_element_type=F32, precision=HIGHEST) + b_ref[...]


def _modulation(conds, w_mod, b_mod):
    n, d = conds.shape
    cols = w_mod.shape[1]
    tn = 512
    return pl.pallas_call(
        _mod_kernel,
        grid=(cols // tn,),
        in_specs=[pl.BlockSpec((n, d), lambda j: (0, 0)),
                  pl.BlockSpec((d, tn), lambda j: (0, j)),
                  pl.BlockSpec((1, tn), lambda j: (0, j))],
        out_specs=pl.BlockSpec((n, tn), lambda j: (0, j)),
        out_shape=jax.ShapeDtypeStruct((n, cols), F32),
        name="mod",
    )(conds, w_mod, b_mod.reshape(1, cols))


def _repack_kernel(tbl_ref, w_ref, o_ref):
    del tbl_ref
    o_ref[...] = w_ref[...].T.astype(BF16)


def _repack_weights(w_t, src_rows, tn):
    _, d = w_t.shape
    nblk = len(src_rows)
    grid_spec = pltpu.PrefetchScalarGridSpec(
        num_scalar_prefetch=1,
        grid=(nblk,),
        in_specs=[pl.BlockSpec((pl.Element(tn), pl.Element(d)),
                               lambda j, tbl: (pl.multiple_of(tbl[j], F32_SUBLANES), 0))],
        out_specs=pl.BlockSpec((d, tn), lambda j, tbl: (0, j)),
    )
    return pl.pallas_call(
        _repack_kernel,
        grid_spec=grid_spec,
        out_shape=jax.ShapeDtypeStruct((d, nblk * tn), BF16),
        name="repack",
    )(jnp.asarray(src_rows, jnp.int32), w_t)


def _proj_kernel(x_ref, xp_ref, xn_ref, mod_ref, gpre_ref, w_ref, wab_ref, wconv_ref, y_ref, ab_ref, *,
                 kinds, col0, tiles_per_seq):
    tm = x_ref.shape[0]
    halo = xp_ref.shape[0]
    pos = lax.rem(pl.program_id(0), tiles_per_seq)
    xa = jnp.concatenate([xp_ref[...], x_ref[...], xn_ref[...]], axis=0)
    h = _prenorm(xa, gpre_ref[...], mod_ref[...])
    hb = h.astype(BF16)
    rows = lax.broadcasted_iota(jnp.int32, (tm, 1), 0)
    keep_dn = (rows != 0) | (pos != 0)
    keep_up = (rows != tm - 1) | (pos != tiles_per_seq - 1)
    for n, kind in enumerate(kinds):
        sl = slice(n * D_MODEL, (n + 1) * D_MODEL)
        wsl = slice(col0 + n * D_MODEL, col0 + (n + 1) * D_MODEL)
        p = jnp.dot(hb, w_ref[:, wsl], preferred_element_type=F32)
        wc = wconv_ref[:, wsl]
        p_dn = jnp.where(keep_dn, pltpu.roll(p, 1, axis=0)[halo:halo + tm], 0.0)
        p_up = jnp.where(keep_up, pltpu.roll(p, tm + 2 * halo - 1, axis=0)[halo:halo + tm], 0.0)
        y = _silu(p_dn * wc[0:1, :] + p[halo:halo + tm] * wc[1:2, :] + p_up * wc[2:3, :])
        if kind == "v":
            y_ref[:, sl] = y.astype(BF16)
            continue
        scale = HEAD_DIM ** -0.5 if kind == "q" else 1.0
        for hd in range(HEADS):
            yh = y[:, hd * HEAD_DIM:(hd + 1) * HEAD_DIM]
            inv = lax.rsqrt(jnp.sum(yh * yh, axis=-1, keepdims=True) + EPS) * scale
            y_ref[:, n * D_MODEL + hd * HEAD_DIM:n * D_MODEL + (hd + 1) * HEAD_DIM] = (yh * inv).astype(BF16)
    h_hi, h_lo = _split_bf16(h[halo:halo + tm], 2)
    hw = jnp.dot(h_hi, wab_ref[...], preferred_element_type=F32)
    ab_ref[...] = (hw[:, 0:LANES] + hw[:, LANES:2 * LANES]
                   + jnp.dot(h_lo, wab_ref[:, 0:LANES], preferred_element_type=F32))


def _project_qkv(x2d, mod3, tiles_per_mod, g_pre, w_pack, w_ab, w_conv, tm, seq, kinds):
    rows, d = x2d.shape
    nq = len(kinds) * d
    nw = 3 * d
    halo = F32_SUBLANES
    hb = tm // halo
    last_halo = rows // halo - 1
    kern = functools.partial(_proj_kernel, kinds=kinds, col0=nw - nq, tiles_per_seq=seq // tm)
    return pl.pallas_call(
        kern,
        grid=(rows // tm,),
        in_specs=[pl.BlockSpec((tm, d), lambda i: (i, 0)),
                  pl.BlockSpec((halo, d), lambda i: (jnp.maximum(i * hb - 1, 0), 0)),
                  pl.BlockSpec((halo, d), lambda i: (jnp.minimum((i + 1) * hb, last_halo), 0)),
                  pl.BlockSpec((None, 1, 3 * d), lambda i: (i // tiles_per_mod, 0, 0)),
                  pl.BlockSpec((1, d), lambda i: (0, 0)),
                  pl.BlockSpec((d, nw), lambda i: (0, 0)),
                  pl.BlockSpec((d, 2 * LANES), lambda i: (0, 0)),
                  pl.BlockSpec((3, nw), lambda i: (0, 0))],
        out_specs=[pl.BlockSpec((tm, nq), lambda i: (i, 0)),
                   pl.BlockSpec((tm, LANES), lambda i: (i, 0))],
        out_shape=[jax.ShapeDtypeStruct((rows, nq), BF16),
                   jax.ShapeDtypeStruct((rows, LANES), F32)],
        compiler_params=pltpu.CompilerParams(vmem_limit_bytes=VMEM_LIMIT),
        name="proj",
    )(x2d, x2d, x2d, mod3, g_pre, w_pack, w_ab, w_conv)


def _unit_tri_inverse(a_list, eye, ii, jj):
    c = a_list[0].shape[0]
    shift = BASE_BLOCK.bit_length() - 1
    same = (ii >> shift) == (jj >> shift)
    d1 = [jnp.where(same, a, 0.0).astype(BF16) for a in a_list]
    d2 = [jnp.dot(d, d, preferred_element_type=F32) for d in d1]
    d2b = [d.astype(BF16) for d in d2]
    z = [jnp.dot((eye - x.astype(F32)).astype(BF16), (eye + y).astype(BF16), preferred_element_type=F32)
         for x, y in zip(d1, d2)]
    d4 = [jnp.dot(d, d, preferred_element_type=F32) for d in d2b]
    t = [x + _bdot(x, y) for x, y in zip(z, d4)]
    b = BASE_BLOCK
    while b < c:
        s1 = b.bit_length() - 1
        off = ((ii >> (s1 + 1)) == (jj >> (s1 + 1))) & ((ii >> s1) != (jj >> s1))
        tb = [x.astype(BF16) for x in t]
        tc = [jnp.dot(x, jnp.where(off, a, 0.0).astype(BF16), preferred_element_type=F32)
              for x, a in zip(tb, a_list)]
        t = [x - jnp.dot(y.astype(BF16), xb, preferred_element_type=F32) for x, y, xb in zip(t, tc, tb)]
        b *= 2
    return t


def _gdn_kernel(*refs, reverse, with_q, accumulate, n_t, tg):
    it = iter(refs)
    y_ref, ab_ref, arow_ref, dtrow_ref, s0_ref = next(it), next(it), next(it), next(it), next(it)
    oin_ref = next(it) if accumulate else None
    o_ref = next(it) if with_q else None
    sfin_ref = None if with_q else next(it)
    s_scr = next(it)

    c = GDN_CHUNK
    i = pl.program_id(1)
    ncol = y_ref.shape[1]
    koff = ncol - 2 * D_MODEL
    voff = ncol - D_MODEL

    @pl.when(i == 0)
    def _():
        s_scr[...] = s0_ref[...]

    ab = ab_ref[...]
    g_cmp = -jnp.exp(arow_ref[...]) * _softplus(ab + dtrow_ref[...])
    beta_cmp = _sigmoid(ab)
    ti = lax.broadcasted_iota(jnp.int32, (tg, tg), 0)
    tj = lax.broadcasted_iota(jnp.int32, (tg, tg), 1)
    cs = c.bit_length() - 1
    same_chunk = (ti >> cs) == (tj >> cs)
    order = (tj >= ti) if reverse else (tj <= ti)
    cum_mat = jnp.where(same_chunk & order, 1.0, 0.0).astype(BF16)
    gcum3 = jnp.dot(cum_mat, jnp.concatenate(_split_bf16(g_cmp, 3), axis=1), preferred_element_type=F32)
    gcum_cmp = gcum3[:, 0:LANES] + gcum3[:, LANES:2 * LANES] + gcum3[:, 2 * LANES:3 * LANES]
    gcum_t = gcum_cmp.T
    eg_cmp = jnp.exp(gcum_cmp)
    dir_off = HEADS if reverse else 0

    ii = lax.broadcasted_iota(jnp.int32, (c, c), 0)
    jj = lax.broadcasted_iota(jnp.int32, (c, c), 1)
    eye = jnp.where(ii == jj, 1.0, 0.0).astype(F32)
    incl = (ii <= jj) if reverse else (ii >= jj)
    strict = (ii < jj) if reverse else (ii > jj)

    n_chunks = tg // c
    chunk_order = list(range(n_chunks - 1, -1, -1) if reverse else range(n_chunks))
    inst = [(cidx, h) for cidx in chunk_order for h in range(HEADS)]

    knb, qsb, be, gc, eg, decay = [], [], [], [], [], []
    for cidx, h in inst:
        rs = slice(cidx * c, (cidx + 1) * c)
        knb.append(y_ref[rs, koff + h * HEAD_DIM:koff + (h + 1) * HEAD_DIM])
        if with_q:
            qsb.append(y_ref[rs, h * HEAD_DIM:(h + 1) * HEAD_DIM])
        la = dir_off + h
        lb = 2 * HEADS + dir_off + h
        be.append(jnp.broadcast_to(beta_cmp[rs, lb:lb + 1], (c, HEAD_DIM)))
        g_col = jnp.broadcast_to(gcum_cmp[rs, la:la + 1], (c, HEAD_DIM))
        g_row = gcum_t[la:la + 1, cidx * c:(cidx + 1) * c]
        gc.append(g_col)
        decay.append(jnp.exp(jnp.where(incl, g_col - g_row, NEG_BIG)))
        eg.append(jnp.broadcast_to(eg_cmp[rs, la:la + 1], (c, HEAD_DIM)))
    kn = [k.astype(F32) for k in knb]
    kb = [k * b for k, b in zip(kn, be)]
    nt_dims = (((1,), (1,)), ((), ()))
    if with_q:
        qs = [q.astype(F32) for q in qsb]
        kk = [lax.dot_general(jnp.concatenate([a.astype(BF16), q], axis=0), k, nt_dims,
                              preferred_element_type=F32) for a, q, k in zip(kb, qsb, knb)]
        attn = [(x[c:2 * c] * d).astype(BF16) for x, d in zip(kk, decay)]
    else:
        kk = [lax.dot_general(a.astype(BF16), k, nt_dims, preferred_element_type=F32)
              for a, k in zip(kb, knb)]
    a_mat = [jnp.where(strict, x[0:c] * d, 0.0) for x, d in zip(kk, decay)]
    t_inv = _unit_tri_inverse(a_mat, eye, ii, jj)
    sol = []
    for n, (cidx, h) in enumerate(inst):
        rs = slice(cidx * c, (cidx + 1) * c)
        vv = y_ref[rs, voff + h * HEAD_DIM:voff + (h + 1) * HEAD_DIM].astype(F32)
        rhs = jnp.concatenate([vv * be[n], kb[n] * eg[n]], axis=1)
        sol.append(_bdot(t_inv[n], rhs))

    for ci in range(n_chunks):
        base = ci * HEADS
        idx = range(base, base + HEADS)
        rs = slice(chunk_order[ci] * c, (chunk_order[ci] + 1) * c)
        s_old = [s_scr[h] for h in range(HEADS)]
        s_b = [s.astype(BF16) for s in s_old]
        if with_q:
            ws = [jnp.dot(jnp.concatenate([sol[n][:, HEAD_DIM:], qs[n] * eg[n]], axis=0).astype(BF16), sb,
                          preferred_element_type=F32) for n, sb in zip(idx, s_b)]
        else:
            ws = [jnp.dot(sol[n][:, HEAD_DIM:].astype(BF16), sb, preferred_element_type=F32)
                  for n, sb in zip(idx, s_b)]
        v_new = [(sol[n][:, 0:HEAD_DIM] - w[0:c]).astype(BF16) for n, w in zip(idx, ws)]
        last = rs.start if reverse else rs.stop - 1
        gl_cmp = gcum_cmp[last:last + 1, :]
        tail_cmp = jnp.exp(gl_cmp - gcum_cmp[rs, :])
        egl_cmp = jnp.exp(gl_cmp)
        k_tail = [(kn[n] * jnp.broadcast_to(tail_cmp[:, dir_off + h:dir_off + h + 1], (c, HEAD_DIM))).T
                  .astype(BF16) for h, n in enumerate(idx)]
        kv = [jnp.dot(kt, v, preferred_element_type=F32) for kt, v in zip(k_tail, v_new)]
        for h in range(HEADS):
            s_decay = jnp.broadcast_to(egl_cmp[:, dir_off + h:dir_off + h + 1], (HEAD_DIM, HEAD_DIM))
            s_scr[h] = s_old[h] * s_decay + kv[h]
        if with_q:
            av = [jnp.dot(attn[n], v, preferred_element_type=F32) for n, v in zip(idx, v_new)]
            for h in range(HEADS):
                hs = slice(h * HEAD_DIM, (h + 1) * HEAD_DIM)
                o = ws[h][c:2 * c] + av[h]
                if accumulate:
                    o = o + oin_ref[rs, hs]
                o_ref[rs, hs] = o

    if not with_q:
        @pl.when(i == n_t - 1)
        def _():
            sfin_ref[...] = s_scr[...]


def _gdn_scan(y, ab, a_row, dt_row, s0, o_in, *, batch, reverse, with_q, tg):
    rows, ncol = y.shape
    n_t = rows // batch // tg
    accumulate = o_in is not None

    def tile(b, i):
        return (b * n_t + ((n_t - 1 - i) if reverse else i), 0)

    state = lambda b, i: (b, 0, 0, 0)
    const = lambda b, i: (0, 0)
    in_specs = [
        pl.BlockSpec((tg, ncol), tile),
        pl.BlockSpec((tg, LANES), tile),
        pl.BlockSpec((1, LANES), const),
        pl.BlockSpec((1, LANES), const),
        pl.BlockSpec((None, HEADS, HEAD_DIM, HEAD_DIM), state),
    ]
    args = [y, ab, a_row, dt_row, s0]
    if accumulate:
        in_specs.append(pl.BlockSpec((tg, D_MODEL), tile))
        args.append(o_in)
    if with_q:
        out_specs = pl.BlockSpec((tg, D_MODEL), tile)
        out_shape = jax.ShapeDtypeStruct((rows, D_MODEL), F32)
    else:
        out_specs = pl.BlockSpec((None, HEADS, HEAD_DIM, HEAD_DIM), state)
        out_shape = jax.ShapeDtypeStruct((batch, HEADS, HEAD_DIM, HEAD_DIM), F32)
    kern = functools.partial(_gdn_kernel, reverse=reverse, with_q=with_q, accumulate=accumulate,
                             n_t=n_t, tg=tg)
    return pl.pallas_call(
        kern,
        grid=(batch, n_t),
        in_specs=in_specs,
        out_specs=out_specs,
        out_shape=out_shape,
        scratch_shapes=[pltpu.VMEM((HEADS, HEAD_DIM, HEAD_DIM), F32)],
        compiler_params=pltpu.CompilerParams(dimension_semantics=("arbitrary", "arbitrary"),
                                             vmem_limit_bytes=VMEM_LIMIT),
        name="gdn_bwd" if reverse else "gdn_fwd",
    )(*args)


def _brancha_kernel(x_ref, mod_ref, gpre_ref, wa_ref, lng_ref, lnb_ref, wsp_ref, bsp_ref, wpa_ref, ma_ref):
    d = D_MODEL
    tm = x_ref.shape[0]
    hb = _prenorm(x_ref[...], gpre_ref[...], mod_ref[...]).astype(BF16)
    vg = _gelu_tanh(jnp.dot(hb, wa_ref[:, d:2 * d], preferred_element_type=F32))
    mu = jnp.mean(vg, axis=-1, keepdims=True)
    vc = vg - mu
    var = jnp.mean(vc * vc, axis=-1, keepdims=True)
    v = (vc * lax.rsqrt(var + EPS) * lng_ref[...] + lnb_ref[...]).astype(BF16)
    parts = []
    for n in range(tm // GM_CHUNK):
        rs = slice(n * GM_CHUNK, (n + 1) * GM_CHUNK)
        cols = [jnp.dot(wsp_ref[g], v[rs, g * HEAD_DIM:(g + 1) * HEAD_DIM], preferred_element_type=F32)
                for g in range(d // HEAD_DIM)]
        parts.append(jnp.concatenate(cols, axis=1) + bsp_ref[...])
    s = jnp.concatenate(parts, axis=0)
    u = _gelu_tanh(jnp.dot(hb, wa_ref[:, 0:d], preferred_element_type=F32))
    z = _silu(jnp.dot(hb, wa_ref[:, 2 * d:3 * d], preferred_element_type=F32))
    ya = (u * s * z).astype(BF16)
    gate = _sigmoid(jnp.dot(hb, wa_ref[:, 3 * d:4 * d], preferred_element_type=F32))
    ma_ref[...] = gate * jnp.dot(ya, wpa_ref[...], preferred_element_type=F32)


def _branch_a(x2d, mod3, tiles_per_mod, g_pre, w_pack, ln_g, ln_b, w_sp, b_sp_e, w_pa, tm):
    rows, d = x2d.shape
    const = lambda i: (0, 0)
    return pl.pallas_call(
        _brancha_kernel,
        grid=(rows // tm,),
        in_specs=[pl.BlockSpec((tm, d), lambda i: (i, 0)),
                  pl.BlockSpec((None, 1, 3 * d), lambda i: (i // tiles_per_mod, 0, 0)),
                  pl.BlockSpec((1, d), const),
                  pl.BlockSpec((d, 4 * d), lambda i: (0, 1)),
                  pl.BlockSpec((1, d), const),
                  pl.BlockSpec((1, d), const),
                  pl.BlockSpec(w_sp.shape, lambda i: (0, 0, 0)),
                  pl.BlockSpec((GM_CHUNK, d), const),
                  pl.BlockSpec((d, d), const)],
        out_specs=pl.BlockSpec((tm, d), lambda i: (i, 0)),
        out_shape=jax.ShapeDtypeStruct((rows, d), F32),
        compiler_params=pltpu.CompilerParams(vmem_limit_bytes=VMEM_LIMIT),
        name="brancha",
    )(x2d, mod3, g_pre, w_pack, ln_g, ln_b, w_sp, b_sp_e, w_pa)


def _out_kernel(x_ref, mod_ref, gpre_ref, gpost_ref, o_ref, ma_ref, wzb_ref, wgb_ref, wpb_ref, wout_ref,
                gon_ref, out_ref):
    d = D_MODEL
    x = x_ref[...]
    mod = mod_ref[...]
    hb = _prenorm(x, gpre_ref[...], mod).astype(BF16)
    zb = _silu(jnp.dot(hb, wzb_ref[...], preferred_element_type=F32))
    gon = gon_ref[...]
    parts = []
    for h in range(HEADS):
        oh = o_ref[:, h * HEAD_DIM:(h + 1) * HEAD_DIM]
        ms = jnp.mean(oh * oh, axis=-1, keepdims=True)
        parts.append(oh * lax.rsqrt(ms + EPS) * gon)
    yb = (jnp.concatenate(parts, axis=1) * zb).astype(BF16)
    gate_b = _sigmoid(jnp.dot(hb, wgb_ref[...], preferred_element_type=F32))
    merged = ma_ref[...] + gate_b * jnp.dot(yb, wpb_ref[...], preferred_element_type=F32)
    z = jnp.dot(merged.astype(BF16), wout_ref[...], preferred_element_type=F32)
    zn = z * lax.rsqrt(jnp.mean(z * z, axis=-1, keepdims=True) + EPS) * gpost_ref[...]
    out_ref[...] = x + mod[:, 2 * d:3 * d] * zn


def _merge_out(x2d, mod3, tiles_per_mod, g_pre, g_post, o, m_a, w_pack, w_pb, w_out, g_onorm, tm):
    rows, d = x2d.shape
    const = lambda i: (0, 0)
    row = lambda i: (i, 0)
    return pl.pallas_call(
        _out_kernel,
        grid=(rows // tm,),
        in_specs=[pl.BlockSpec((tm, d), row),
                  pl.BlockSpec((None, 1, 3 * d), lambda i: (i // tiles_per_mod, 0, 0)),
                  pl.BlockSpec((1, d), const),
                  pl.BlockSpec((1, d), const),
                  pl.BlockSpec((tm, d), row),
                  pl.BlockSpec((tm, d), row),
                  pl.BlockSpec((d, d), lambda i: (0, 3)),
                  pl.BlockSpec((d, d), lambda i: (0, 8)),
                  pl.BlockSpec((d, d), const),
                  pl.BlockSpec((d, d), const),
                  pl.BlockSpec((1, HEAD_DIM), const)],
        out_specs=pl.BlockSpec((tm, d), row),
        out_shape=jax.ShapeDtypeStruct((rows, d), F32),
        compiler_params=pltpu.CompilerParams(vmem_limit_bytes=VMEM_LIMIT),
        name="merge_out",
    )(x2d, mod3, g_pre, g_post, o, m_a, w_pack, w_pack, w_pb, w_out, g_onorm)


def _layer(x, ctx, mod_x, mod_c, g_pre, g_post, w_in, w_conv, a_log, dt_bias, g_onorm, gm_ln_g, gm_ln_b,
           w_sp, b_sp, w_pa, w_pb, w_out):
    batch, seq, d = x.shape
    ctx_len = ctx.shape[1]
    kd = HEADS * HEAD_DIM
    off_a = 3 * kd
    off_zb = off_a + 4 * HEADS

    tn = 512
    w_t = jnp.transpose(w_in)
    w_pack = _repack_weights(w_t, list(range(0, off_a, tn)) + list(range(off_zb, w_in.shape[1], tn)), tn)
    w_ab = jnp.pad(jnp.transpose(w_t[off_a:off_zb]), ((0, 0), (0, LANES - 4 * HEADS)))
    w_ab2 = jnp.concatenate(_split_bf16(w_ab, 2), axis=1)
    a_row = jnp.pad(a_log.reshape(1, 2 * HEADS), ((0, 0), (0, LANES - 2 * HEADS)))
    dt_row = jnp.pad(dt_bias.reshape(1, 2 * HEADS), ((0, 0), (0, LANES - 2 * HEADS)))
    b_sp_e = jnp.repeat(b_sp.T, HEAD_DIM, axis=1)
    g_pre2 = g_pre.reshape(1, d)

    x2d = x.reshape(batch * seq, d)
    c2d = ctx.reshape(batch * ctx_len, d)
    tm = 512
    tm2 = 2 * tm

    kv_c, ab_c = _project_qkv(c2d, mod_c, 1 << 30, g_pre2, w_pack, w_ab2, w_conv, ctx_len, ctx_len, ("k", "v"))
    s_zero = jnp.zeros((batch, HEADS, HEAD_DIM, HEAD_DIM), F32)
    s_fwd = _gdn_scan(kv_c, ab_c, a_row, dt_row, s_zero, None,
                      batch=batch, reverse=False, with_q=False, tg=ctx_len)
    s_bwd = _gdn_scan(kv_c, ab_c, a_row, dt_row, s_zero, None,
                      batch=batch, reverse=True, with_q=False, tg=ctx_len)

    qkv, ab = _project_qkv(x2d, mod_x, seq // tm, g_pre2, w_pack, w_ab2, w_conv, tm, seq, ("q", "k", "v"))
    tg = 2 * GDN_CHUNK
    o_f = _gdn_scan(qkv, ab, a_row, dt_row, s_fwd, None, batch=batch, reverse=False, with_q=True, tg=tg)
    o = _gdn_scan(qkv, ab, a_row, dt_row, s_bwd, o_f, batch=batch, reverse=True, with_q=True, tg=tg)
    m_a = _branch_a(x2d, mod_x, seq // tm2, g_pre2, w_pack, gm_ln_g.reshape(1, d), gm_ln_b.reshape(1, d),
                    w_sp.astype(BF16), b_sp_e, w_pa.astype(BF16), tm2)
    out = _merge_out(x2d, mod_x, seq // tm2, g_pre2, g_post.reshape(1, d), o, m_a, w_pack,
                     w_pb.astype(BF16), w_out.astype(BF16), g_onorm.reshape(1, HEAD_DIM), tm2)
    return out.reshape(batch, seq, d)


def kernel(x, c, ctx, c_ctx, w_mod, b_mod, g_pre, g_post, w_in, w_conv, a_log, dt_bias, g_onorm, gm_ln_g,
           gm_ln_b, w_sp, b_sp, w_pa, w_pb, w_out):
    batch, _, d = x.shape
    depth = w_mod.shape[0]
    assert depth == 1, "context tokens are only read (never updated) by a single-layer stack"
    assert ctx.shape[1] % GDN_CHUNK == 0 and x.shape[1] % (8 * GDN_CHUNK) == 0 and d == D_MODEL
    pad = (-(batch + 1)) % 8
    conds = jnp.concatenate([c, c_ctx[None, :], jnp.zeros((pad, d), F32)], axis=0)
    i = 0
    m = _modulation(conds, w_mod[i], b_mod[i])
    mod_x = m[0:batch].reshape(batch, 1, 3 * d)
    mod_c = m[batch:batch + 1].reshape(1, 1, 3 * d)
    return _layer(x, ctx, mod_x, mod_c, g_pre[i], g_post[i], w_in[i], w_conv[i], a_log[i], dt_bias[i],
                  g_onorm[i], gm_ln_g[i], gm_ln_b[i], w_sp[i], b_sp[i], w_pa[i], w_pb[i], w_out[i])
```

```python
import functools

import jax
import jax.numpy as jnp
from jax import lax
from jax.experimental import pallas as pl
from jax.experimental.pallas import tpu as pltpu

F32 = jnp.float32
BF16 = jnp.bfloat16
HIGHEST = lax.Precision.HIGHEST

D_MODEL = 1024
HEADS = 8
HEAD_DIM = 128
GM_CHUNK = 128
EPS = 1e-6
NEG_BIG = -1e30

LANES = 128
F32_SUBLANES = 8
GDN_CHUNK = 128
BASE_BLOCK = 8
VMEM_LIMIT = 56 * 1024 * 1024


def _sigmoid(x):
    return 1.0 / (1.0 + jnp.exp(-x))


def _silu(x):
    return x * _sigmoid(x)


def _gelu_tanh(x):
    c = 0.7978845608028654
    return 0.5 * x * (1.0 + jnp.tanh(c * (x + 0.044715 * (x * x * x))))


def _softplus(x):
    return jnp.maximum(x, 0.0) + jnp.log1p(jnp.exp(-jnp.abs(x)))


def _prenorm(x, g_pre, mod):
    ms = jnp.mean(x * x, axis=-1, keepdims=True)
    y = x * lax.rsqrt(ms + EPS) * g_pre
    return y * (1.0 + mod[:, D_MODEL:2 * D_MODEL]) + mod[:, 0:D_MODEL]


def _bdot(a, b):
    return jnp.dot(a.astype(BF16), b.astype(BF16), preferred_element_type=F32)


def _split_bf16(x, parts):
    out, r = [], x
    for _ in range(parts):
        p = r.astype(BF16)
        out.append(p)
        r = r - p.astype(F32)
    return out


def _mod_kernel(c_ref, w_ref, b_ref, o_ref):
    s = _silu(c_ref[...])
    o_ref[...] = jnp.dot(s, w_ref[...], preferred_element_type=F32, precision=HIGHEST) + b_ref[...]


def _modulation(conds, w_mod, b_mod):
    n, d = conds.shape
    cols = w_mod.shape[1]
    tn = 512
    return pl.pallas_call(
        _mod_kernel,
        grid=(cols // tn,),
        in_specs=[pl.BlockSpec((n, d), lambda j: (0, 0)),
                  pl.BlockSpec((d, tn), lambda j: (0, j)),
                  pl.BlockSpec((1, tn), lambda j: (0, j))],
        out_specs=pl.BlockSpec((n, tn), lambda j: (0, j)),
        out_shape=jax.ShapeDtypeStruct((n, cols), F32),
        name="mod",
    )(conds, w_mod, b_mod.reshape(1, cols))


def _repack_kernel(tbl_ref, w_ref, o_ref):
    del tbl_ref
    o_ref[...] = w_ref[...].T.astype(BF16)


def _repack_weights(w_t, src_rows, tn):
    _, d = w_t.shape
    nblk = len(src_rows)
    grid_spec = pltpu.PrefetchScalarGridSpec(
        num_scalar_prefetch=1,
        grid=(nblk,),
        in_specs=[pl.BlockSpec((pl.Element(tn), pl.Element(d)),
                               lambda j, tbl: (pl.multiple_of(tbl[j], F32_SUBLANES), 0))],
        out_specs=pl.BlockSpec((d, tn), lambda j, tbl: (0, j)),
    )
    return pl.pallas_call(
        _repack_kernel,
        grid_spec=grid_spec,
        out_shape=jax.ShapeDtypeStruct((d, nblk * tn), BF16),
        name="repack",
    )(jnp.asarray(src_rows, jnp.int32), w_t)


def _repack_logits_kernel(w_ref, o_ref, *, n_cols):
    wt = w_ref[...].T
    lane = lax.broadcasted_iota(jnp.int32, wt.shape, 1)
    hi, lo = _split_bf16(jnp.where(lane < n_cols, wt, 0.0), 2)
    o_ref[:, 0:LANES] = hi
    o_ref[:, LANES:2 * LANES] = lo


def _repack_logit_weights(w_t, row0, n_cols):
    _, d = w_t.shape
    return pl.pallas_call(
        functools.partial(_repack_logits_kernel, n_cols=n_cols),
        grid=(1,),
        in_specs=[pl.BlockSpec((pl.Element(LANES), pl.Element(d)), lambda j: (row0, 0))],
        out_specs=pl.BlockSpec((d, 2 * LANES), lambda j: (0, 0)),
        out_shape=jax.ShapeDtypeStruct((d, 2 * LANES), BF16),
        name="repack_logits",
    )(w_t)


def _proj_kernel(x_ref, xp_ref, xn_ref, mod_ref, gpre_ref, w_ref, wab_ref, wconv_ref, y_ref, ab_ref, *,
                 kinds, col0, tiles_per_seq):
    tm = x_ref.shape[0]
    halo = xp_ref.shape[0]
    pos = lax.rem(pl.program_id(0), tiles_per_seq)
    xa = jnp.concatenate([xp_ref[...], x_ref[...], xn_ref[...]], axis=0)
    h = _prenorm(xa, gpre_ref[...], mod_ref[...])
    hb = h.astype(BF16)
    rows = lax.broadcasted_iota(jnp.int32, (tm, 1), 0)
    keep_dn = (rows != 0) | (pos != 0)
    keep_up = (rows != tm - 1) | (pos != tiles_per_seq - 1)
    for n, kind in enumerate(kinds):
        sl = slice(n * D_MODEL, (n + 1) * D_MODEL)
        wsl = slice(col0 + n * D_MODEL, col0 + (n + 1) * D_MODEL)
        p = jnp.dot(hb, w_ref[:, wsl], preferred_element_type=F32)
        wc = wconv_ref[:, wsl]
        p_dn = jnp.where(keep_dn, pltpu.roll(p, 1, axis=0)[halo:halo + tm], 0.0)
        p_up = jnp.where(keep_up, pltpu.roll(p, tm + 2 * halo - 1, axis=0)[halo:halo + tm], 0.0)
        y = _silu(p_dn * wc[0:1, :] + p[halo:halo + tm] * wc[1:2, :] + p_up * wc[2:3, :])
        if kind == "v":
            y_ref[:, sl] = y.astype(BF16)
            continue
        scale = HEAD_DIM ** -0.5 if kind == "q" else 1.0
        for hd in range(HEADS):
            yh = y[:, hd * HEAD_DIM:(hd + 1) * HEAD_DIM]
            inv = lax.rsqrt(jnp.sum(yh * yh, axis=-1, keepdims=True) + EPS) * scale
            y_ref[:, n * D_MODEL + hd * HEAD_DIM:n * D_MODEL + (hd + 1) * HEAD_DIM] = (yh * inv).astype(BF16)
    h_hi, h_lo = _split_bf16(h[halo:halo + tm], 2)
    hw = jnp.dot(h_hi, wab_ref[...], preferred_element_type=F32)
    ab_ref[...] = (hw[:, 0:LANES] + hw[:, LANES:2 * LANES]
                   + jnp.dot(h_lo, wab_ref[:, 0:LANES], preferred_element_type=F32))


def _project_qkv(x2d, mod3, tiles_per_mod, g_pre, w_pack, w_ab, w_conv, tm, seq, kinds):
    rows, d = x2d.shape
    nq = len(kinds) * d
    nw = 3 * d
    halo = F32_SUBLANES
    hb = tm // halo
    last_halo = rows // halo - 1
    kern = functools.partial(_proj_kernel, kinds=kinds, col0=nw - nq, tiles_per_seq=seq // tm)
    return pl.pallas_call(
        kern,
        grid=(rows // tm,),
        in_specs=[pl.BlockSpec((tm, d), lambda i: (i, 0)),
                  pl.BlockSpec((halo, d), lambda i: (jnp.maximum(i * hb - 1, 0), 0)),
                  pl.BlockSpec((halo, d), lambda i: (jnp.minimum((i + 1) * hb, last_halo), 0)),
                  pl.BlockSpec((None, 1, 3 * d), lambda i: (i // tiles_per_mod, 0, 0)),
                  pl.BlockSpec((1, d), lambda i: (0, 0)),
                  pl.BlockSpec((d, nw), lambda i: (0, 0)),
                  pl.BlockSpec((d, 2 * LANES), lambda i: (0, 0)),
                  pl.BlockSpec((3, nw), lambda i: (0, 0))],
        out_specs=[pl.BlockSpec((tm, nq), lambda i: (i, 0)),
                   pl.BlockSpec((tm, LANES), lambda i: (i, 0))],
        out_shape=[jax.ShapeDtypeStruct((rows, nq), BF16),
                   jax.ShapeDtypeStruct((rows, LANES), F32)],
        compiler_params=pltpu.CompilerParams(vmem_limit_bytes=VMEM_LIMIT),
        name="proj",
    )(x2d, x2d, x2d, mod3, g_pre, w_pack, w_ab, w_conv)


def _unit_tri_inverse(a_list, eye, ii, jj):
    c = a_list[0].shape[0]
    shift = BASE_BLOCK.bit_length() - 1
    same = (ii >> shift) == (jj >> shift)
    d1 = [jnp.where(same, a, 0.0).astype(BF16) for a in a_list]
    d2 = [jnp.dot(d, d, preferred_element_type=F32) for d in d1]
    d2b = [d.astype(BF16) for d in d2]
    z = [jnp.dot((eye - x.astype(F32)).astype(BF16), (eye + y).astype(BF16), preferred_element_type=F32)
         for x, y in zip(d1, d2)]
    d4 = [jnp.dot(d, d, preferred_element_type=F32) for d in d2b]
    t = [x + _bdot(x, y) for x, y in zip(z, d4)]
    b = BASE_BLOCK
    while b < c:
        s1 = b.bit_length() - 1
        off = ((ii >> (s1 + 1)) == (jj >> (s1 + 1))) & ((ii >> s1) != (jj >> s1))
        tb = [x.astype(BF16) for x in t]
        tc = [jnp.dot(x, jnp.where(off, a, 0.0).astype(BF16), preferred_element_type=F32)
              for x, a in zip(tb, a_list)]
        t = [x - jnp.dot(y.astype(BF16), xb, preferred_element_type=F32) for x, y, xb in zip(t, tc, tb)]
        b *= 2
    return t


def _gdn_kernel(*refs, reverse, with_q, accumulate, n_t, tg):
    it = iter(refs)
    y_ref, ab_ref, arow_ref, dtrow_ref, s0_ref = next(it), next(it), next(it), next(it), next(it)
    oin_ref = next(it) if accumulate else None
    o_ref = next(it) if with_q else None
    sfin_ref = None if with_q else next(it)
    s_scr = next(it)

    c = GDN_CHUNK
    i = pl.program_id(1)
    ncol = y_ref.shape[1]
    koff = ncol - 2 * D_MODEL
    voff = ncol - D_MODEL

    @pl.when(i == 0)
    def _():
        s_scr[...] = s0_ref[...]

    ab = ab_ref[...]
    g_cmp = -jnp.exp(arow_ref[...]) * _softplus(ab + dtrow_ref[...])
    beta_cmp = _sigmoid(ab)
    ti = lax.broadcasted_iota(jnp.int32, (tg, tg), 0)
    tj = lax.broadcasted_iota(jnp.int32, (tg, tg), 1)
    cs = c.bit_length() - 1
    same_chunk = (ti >> cs) == (tj >> cs)
    order = (tj >= ti) if reverse else (tj <= ti)
    cum_mat = jnp.where(same_chunk & order, 1.0, 0.0).astype(BF16)
    gcum3 = jnp.dot(cum_mat, jnp.concatenate(_split_bf16(g_cmp, 3), axis=1), preferred_element_type=F32)
    gcum_cmp = gcum3[:, 0:LANES] + gcum3[:, LANES:2 * LANES] + gcum3[:, 2 * LANES:3 * LANES]
    gcum_t = gcum_cmp.T
    eg_cmp = jnp.exp(gcum_cmp)
    dir_off = HEADS if reverse else 0

    ii = lax.broadcasted_iota(jnp.int32, (c, c), 0)
    jj = lax.broadcasted_iota(jnp.int32, (c, c), 1)
    eye = jnp.where(ii == jj, 1.0, 0.0).astype(F32)
    incl = (ii <= jj) if reverse else (ii >= jj)
    strict = (ii < jj) if reverse else (ii > jj)

    n_chunks = tg // c
    chunk_order = list(range(n_chunks - 1, -1, -1) if reverse else range(n_chunks))
    inst = [(cidx, h) for cidx in chunk_order for h in range(HEADS)]

    knb, qsb, be, gc, eg, decay = [], [], [], [], [], []
    for cidx, h in inst:
        rs = slice(cidx * c, (cidx + 1) * c)
        knb.append(y_ref[rs, koff + h * HEAD_DIM:koff + (h + 1) * HEAD_DIM])
        if with_q:
            qsb.append(y_ref[rs, h * HEAD_DIM:(h + 1) * HEAD_DIM])
        la = dir_off + h
        lb = 2 * HEADS + dir_off + h
        be.append(jnp.broadcast_to(beta_cmp[rs, lb:lb + 1], (c, HEAD_DIM)))
        g_col = jnp.broadcast_to(gcum_cmp[rs, la:la + 1], (c, HEAD_DIM))
        g_row = gcum_t[la:la + 1, cidx * c:(cidx + 1) * c]
        gc.append(g_col)
        decay.append(jnp.exp(jnp.where(incl, g_col - g_row, NEG_BIG)))
        eg.append(jnp.broadcast_to(eg_cmp[rs, la:la + 1], (c, HEAD_DIM)))
    kn = [k.astype(F32) for k in knb]
    kb = [k * b for k, b in zip(kn, be)]
    nt_dims = (((1,), (1,)), ((), ()))
    if with_q:
        qs = [q.astype(F32) for q in qsb]
        kk = [lax.dot_general(jnp.concatenate([a.astype(BF16), q], axis=0), k, nt_dims,
                              preferred_element_type=F32) for a, q, k in zip(kb, qsb, knb)]
        attn = [(x[c:2 * c] * d).astype(BF16) for x, d in zip(kk, decay)]
    else:
        kk = [lax.dot_general(a.astype(BF16), k, nt_dims, preferred_element_type=F32)
              for a, k in zip(kb, knb)]
    a_mat = [jnp.where(strict, x[0:c] * d, 0.0) for x, d in zip(kk, decay)]
    t_inv = _unit_tri_inverse(a_mat, eye, ii, jj)
    sol = []
    for n, (cidx, h) in enumerate(inst):
        rs = slice(cidx * c, (cidx + 1) * c)
        vv = y_ref[rs, voff + h * HEAD_DIM:voff + (h + 1) * HEAD_DIM].astype(F32)
        rhs = jnp.concatenate([vv * be[n], kb[n] * eg[n]], axis=1)
        sol.append(_bdot(t_inv[n], rhs))

    for ci in range(n_chunks):
        base = ci * HEADS
        idx = range(base, base + HEADS)
        rs = slice(chunk_order[ci] * c, (chunk_order[ci] + 1) * c)
        s_old = [s_scr[h] for h in range(HEADS)]
        s_b = [s.astype(BF16) for s in s_old]
        if with_q:
            ws = [jnp.dot(jnp.concatenate([sol[n][:, HEAD_DIM:], qs[n] * eg[n]], axis=0).astype(BF16), sb,
                          preferred_element_type=F32) for n, sb in zip(idx, s_b)]
        else:
            ws = [jnp.dot(sol[n][:, HEAD_DIM:].astype(BF16), sb, preferred_element_type=F32)
                  for n, sb in zip(idx, s_b)]
        v_new = [(sol[n][:, 0:HEAD_DIM] - w[0:c]).astype(BF16) for n, w in zip(idx, ws)]
        last = rs.start if reverse else rs.stop - 1
        gl_cmp = gcum_cmp[last:last + 1, :]
        tail_cmp = jnp.exp(gl_cmp - gcum_cmp[rs, :])
        egl_cmp = jnp.exp(gl_cmp)
        k_tail = [(kn[n] * jnp.broadcast_to(tail_cmp[:, dir_off + h:dir_off + h + 1], (c, HEAD_DIM))).T
                  .astype(BF16) for h, n in enumerate(idx)]
        kv = [jnp.dot(kt, v, preferred_element_type=F32) for kt, v in zip(k_tail, v_new)]
        for h in range(HEADS):
            s_decay = jnp.broadcast_to(egl_cmp[:, dir_off + h:dir_off + h + 1], (HEAD_DIM, HEAD_DIM))
            s_scr[h] = s_old[h] * s_decay + kv[h]
        if with_q:
            av = [jnp.dot(attn[n], v, preferred_element_type=F32) for n, v in zip(idx, v_new)]
            for h in range(HEADS):
                hs = slice(h * HEAD_DIM, (h + 1) * HEAD_DIM)
                o = ws[h][c:2 * c] + av[h]
                if accumulate:
                    o = o + oin_ref[rs, hs]
                o_ref[rs, hs] = o

    if not with_q:
        @pl.when(i == n_t - 1)
        def _():
            sfin_ref[...] = s_scr[...]


def _gdn_scan(y, ab, a_row, dt_row, s0, o_in, *, batch, reverse, with_q, tg):
    rows, ncol = y.shape
    n_t = rows // batch // tg
    accumulate = o_in is not None

    def tile(b, i):
        return (b * n_t + ((n_t - 1 - i) if reverse else i), 0)

    state = lambda b, i: (b, 0, 0, 0)
    const = lambda b, i: (0, 0)
    in_specs = [
        pl.BlockSpec((tg, ncol), tile),
        pl.BlockSpec((tg, LANES), tile),
        pl.BlockSpec((1, LANES), const),
        pl.BlockSpec((1, LANES), const),
        pl.BlockSpec((None, HEADS, HEAD_DIM, HEAD_DIM), state),
    ]
    args = [y, ab, a_row, dt_row, s0]
    if accumulate:
        in_specs.append(pl.BlockSpec((tg, D_MODEL), tile))
        args.append(o_in)
    if with_q:
        out_specs = pl.BlockSpec((tg, D_MODEL), tile)
        out_shape = jax.ShapeDtypeStruct((rows, D_MODEL), F32)
    else:
        out_specs = pl.BlockSpec((None, HEADS, HEAD_DIM, HEAD_DIM), state)
        out_shape = jax.ShapeDtypeStruct((batch, HEADS, HEAD_DIM, HEAD_DIM), F32)
    kern = functools.partial(_gdn_kernel, reverse=reverse, with_q=with_q, accumulate=accumulate,
                             n_t=n_t, tg=tg)
    return pl.pallas_call(
        kern,
        grid=(batch, n_t),
        in_specs=in_specs,
        out_specs=out_specs,
        out_shape=out_shape,
        scratch_shapes=[pltpu.VMEM((HEADS, HEAD_DIM, HEAD_DIM), F32)],
        compiler_params=pltpu.CompilerParams(dimension_semantics=("arbitrary", "arbitrary"),
                                             vmem_limit_bytes=VMEM_LIMIT),
        name="gdn_bwd" if reverse else "gdn_fwd",
    )(*args)


def _brancha_kernel(x_ref, mod_ref, gpre_ref, wa_ref, lng_ref, lnb_ref, wsp_ref, bsp_ref, wpa_ref, ma_ref):
    d = D_MODEL
    tm = x_ref.shape[0]
    hb = _prenorm(x_ref[...], gpre_ref[...], mod_ref[...]).astype(BF16)
    vg = _gelu_tanh(jnp.dot(hb, wa_ref[:, d:2 * d], preferred_element_type=F32))
    mu = jnp.mean(vg, axis=-1, keepdims=True)
    vc = vg - mu
    var = jnp.mean(vc * vc, axis=-1, keepdims=True)
    v = (vc * lax.rsqrt(var + EPS) * lng_ref[...] + lnb_ref[...]).astype(BF16)
    parts = []
    for n in range(tm // GM_CHUNK):
        rs = slice(n * GM_CHUNK, (n + 1) * GM_CHUNK)
        cols = [jnp.dot(wsp_ref[g], v[rs, g * HEAD_DIM:(g + 1) * HEAD_DIM], preferred_element_type=F32)
                for g in range(d // HEAD_DIM)]
        parts.append(jnp.concatenate(cols, axis=1) + bsp_ref[...])
    s = jnp.concatenate(parts, axis=0)
    u = _gelu_tanh(jnp.dot(hb, wa_ref[:, 0:d], preferred_element_type=F32))
    z = _silu(jnp.dot(hb, wa_ref[:, 2 * d:3 * d], preferred_element_type=F32))
    ya = (u * s * z).astype(BF16)
    gate = _sigmoid(jnp.dot(hb, wa_ref[:, 3 * d:4 * d], preferred_element_type=F32))
    ma_ref[...] = gate * jnp.dot(ya, wpa_ref[...], preferred_element_type=F32)


def _branch_a(x2d, mod3, tiles_per_mod, g_pre, w_pack, ln_g, ln_b, w_sp, b_sp_e, w_pa, tm):
    rows, d = x2d.shape
    const = lambda i: (0, 0)
    return pl.pallas_call(
        _brancha_kernel,
        grid=(rows // tm,),
        in_specs=[pl.BlockSpec((tm, d), lambda i: (i, 0)),
                  pl.BlockSpec((None, 1, 3 * d), lambda i: (i // tiles_per_mod, 0, 0)),
                  pl.BlockSpec((1, d), const),
                  pl.BlockSpec((d, 4 * d), lambda i: (0, 1)),
                  pl.BlockSpec((1, d), const),
                  pl.BlockSpec((1, d), const),
                  pl.BlockSpec(w_sp.shape, lambda i: (0, 0, 0)),
                  pl.BlockSpec((GM_CHUNK, d), const),
                  pl.BlockSpec((d, d), const)],
        out_specs=pl.BlockSpec((tm, d), lambda i: (i, 0)),
        out_shape=jax.ShapeDtypeStruct((rows, d), F32),
        compiler_params=pltpu.CompilerParams(vmem_limit_bytes=VMEM_LIMIT),
        name="brancha",
    )(x2d, mod3, g_pre, w_pack, ln_g, ln_b, w_sp, b_sp_e, w_pa)


def _out_kernel(x_ref, mod_ref, gpre_ref, gpost_ref, o_ref, ma_ref, wzb_ref, wgb_ref, wpb_ref, wout_ref,
                gon_ref, out_ref):
    d = D_MODEL
    x = x_ref[...]
    mod = mod_ref[...]
    hb = _prenorm(x, gpre_ref[...], mod).astype(BF16)
    zb = _silu(jnp.dot(hb, wzb_ref[...], preferred_element_type=F32))
    gon = gon_ref[...]
    parts = []
    for h in range(HEADS):
        oh = o_ref[:, h * HEAD_DIM:(h + 1) * HEAD_DIM]
        ms = jnp.mean(oh * oh, axis=-1, keepdims=True)
        parts.append(oh * lax.rsqrt(ms + EPS) * gon)
    yb = (jnp.concatenate(parts, axis=1) * zb).astype(BF16)
    gate_b = _sigmoid(jnp.dot(hb, wgb_ref[...], preferred_element_type=F32))
    merged = ma_ref[...] + gate_b * jnp.dot(yb, wpb_ref[...], preferred_element_type=F32)
    z = jnp.dot(merged.astype(BF16), wout_ref[...], preferred_element_type=F32)
    zn = z * lax.rsqrt(jnp.mean(z * z, axis=-1, keepdims=True) + EPS) * gpost_ref[...]
    out_ref[...] = x + mod[:, 2 * d:3 * d] * zn


def _merge_out(x2d, mod3, tiles_per_mod, g_pre, g_post, o, m_a, w_pack, w_pb, w_out, g_onorm, tm):
    rows, d = x2d.shape
    const = lambda i: (0, 0)
    row = lambda i: (i, 0)
    return pl.pallas_call(
        _out_kernel,
        grid=(rows // tm,),
        in_specs=[pl.BlockSpec((tm, d), row),
                  pl.BlockSpec((None, 1, 3 * d), lambda i: (i // tiles_per_mod, 0, 0)),
                  pl.BlockSpec((1, d), const),
                  pl.BlockSpec((1, d), const),
                  pl.BlockSpec((tm, d), row),
                  pl.BlockSpec((tm, d), row),
                  pl.BlockSpec((d, d), lambda i: (0, 3)),
                  pl.BlockSpec((d, d), lambda i: (0, 8)),
                  pl.BlockSpec((d, d), const),
                  pl.BlockSpec((d, d), const),
                  pl.BlockSpec((1, HEAD_DIM), const)],
        out_specs=pl.BlockSpec((tm, d), row),
        out_shape=jax.ShapeDtypeStruct((rows, d), F32),
        compiler_params=pltpu.CompilerParams(vmem_limit_bytes=VMEM_LIMIT),
        name="merge_out",
    )(x2d, mod3, g_pre, g_post, o, m_a, w_pack, w_pack, w_pb, w_out, g_onorm)


def _layer(x, ctx, mod_x, mod_c, g_pre, g_post, w_in, w_conv, a_log, dt_bias, g_onorm, gm_ln_g, gm_ln_b,
           w_sp, b_sp, w_pa, w_pb, w_out):
    batch, seq, d = x.shape
    ctx_len = ctx.shape[1]
    kd = HEADS * HEAD_DIM
    off_a = 3 * kd
    off_zb = off_a + 4 * HEADS

    tn = 512
    w_t = jnp.transpose(w_in)
    w_pack = _repack_weights(w_t, list(range(0, off_a, tn)) + list(range(off_zb, w_in.shape[1], tn)), tn)
    w_ab2 = _repack_logit_weights(w_t, off_a, off_zb - off_a)
    a_row = jnp.pad(a_log.reshape(1, 2 * HEADS), ((0, 0), (0, LANES - 2 * HEADS)))
    dt_row = jnp.pad(dt_bias.reshape(1, 2 * HEADS), ((0, 0), (0, LANES - 2 * HEADS)))
    b_sp_e = jnp.repeat(b_sp.T, HEAD_DIM, axis=1)
    g_pre2 = g_pre.reshape(1, d)

    x2d = x.reshape(batch * seq, d)
    c2d = ctx.reshape(batch * ctx_len, d)
    tm = 512
    tm2 = 2 * tm

    kv_c, ab_c = _project_qkv(c2d, mod_c, 1 << 30, g_pre2, w_pack, w_ab2, w_conv, ctx_len, ctx_len, ("k", "v"))
    s_zero = jnp.zeros((batch, HEADS, HEAD_DIM, HEAD_DIM), F32)
    s_fwd = _gdn_scan(kv_c, ab_c, a_row, dt_row, s_zero, None,
                      batch=batch, reverse=False, with_q=False, tg=ctx_len)
    s_bwd = _gdn_scan(kv_c, ab_c, a_row, dt_row, s_zero, None,
                      batch=batch, reverse=True, with_q=False, tg=ctx_len)

    qkv, ab = _project_qkv(x2d, mod_x, seq // tm, g_pre2, w_pack, w_ab2, w_conv, tm, seq, ("q", "k", "v"))
    tg = 2 * GDN_CHUNK
    o_f = _gdn_scan(qkv, ab, a_row, dt_row, s_fwd, None, batch=batch, reverse=False, with_q=True, tg=tg)
    o = _gdn_scan(qkv, ab, a_row, dt_row, s_bwd, o_f, batch=batch, reverse=True, with_q=True, tg=tg)
    m_a = _branch_a(x2d, mod_x, seq // tm2, g_pre2, w_pack, gm_ln_g.reshape(1, d), gm_ln_b.reshape(1, d),
                    w_sp.astype(BF16), b_sp_e, w_pa.astype(BF16), tm2)
    out = _merge_out(x2d, mod_x, seq // tm2, g_pre2, g_post.reshape(1, d), o, m_a, w_pack,
                     w_pb.astype(BF16), w_out.astype(BF16), g_onorm.reshape(1, HEAD_DIM), tm2)
    return out.reshape(batch, seq, d)


def kernel(x, c, ctx, c_ctx, w_mod, b_mod, g_pre, g_post, w_in, w_conv, a_log, dt_bias, g_onorm, gm_ln_g,
           gm_ln_b, w_sp, b_sp, w_pa, w_pb, w_out):
    batch, _, d = x.shape
    depth = w_mod.shape[0]
    assert depth == 1, "context tokens are only read (never updated) by a single-layer stack"
    assert ctx.shape[1] % GDN_CHUNK == 0 and x.shape[1] % (8 * GDN_CHUNK) == 0 and d == D_MODEL
    pad = (-(batch + 1)) % 8
    conds = jnp.concatenate([c, c_ctx[None, :], jnp.zeros((pad, d), F32)], axis=0)
    i = 0
    m = _modulation(conds, w_mod[i], b_mod[i])
    mod_x = m[0:batch].reshape(batch, 1, 3 * d)
    mod_c = m[batch:batch + 1].reshape(1, 1, 3 * d)
    return _layer(x, ctx, mod_x, mod_c, g_pre[i], g_post[i], w_in[i], w_conv[i], a_log[i], dt_bias[i],
                  g_onorm[i], gm_ln_g[i], gm_ln_b[i], w_sp[i], b_sp[i], w_pa[i], w_pb[i], w_out[i])
```

```python
import functools

import jax
import jax.numpy as jnp
from jax import lax
from jax.experimental import pallas as pl
from jax.experimental.pallas import tpu as pltpu

F32 = jnp.float32
BF16 = jnp.bfloat16
HIGHEST = lax.Precision.HIGHEST

D_MODEL = 1024
HEADS = 8
HEAD_DIM = 128
GM_CHUNK = 128
EPS = 1e-6
NEG_BIG = -1e30

LANES = 128
F32_SUBLANES = 8
GDN_CHUNK = 128
BASE_BLOCK = 8
VMEM_LIMIT = 56 * 1024 * 1024


def _sigmoid(x):
    return 1.0 / (1.0 + jnp.exp(-x))


def _silu(x):
    return x * _sigmoid(x)


def _gelu_tanh(x):
    c = 0.7978845608028654
    return 0.5 * x * (1.0 + jnp.tanh(c * (x + 0.044715 * (x * x * x))))


def _softplus(x):
    return jnp.maximum(x, 0.0) + jnp.log1p(jnp.exp(-jnp.abs(x)))


def _prenorm(x, g_pre, mod):
    ms = jnp.mean(x * x, axis=-1, keepdims=True)
    y = x * lax.rsqrt(ms + EPS) * g_pre
    return y * (1.0 + mod[:, D_MODEL:2 * D_MODEL]) + mod[:, 0:D_MODEL]


def _bdot(a, b):
    return jnp.dot(a.astype(BF16), b.astype(BF16), preferred_element_type=F32)


def _split_bf16(x, parts):
    out, r = [], x
    for _ in range(parts):
        p = r.astype(BF16)
        out.append(p)
        r = r - p.astype(F32)
    return out


def _mod_kernel(c_ref, w_ref, b_ref, o_ref):
    s = _silu(c_ref[...])
    o_ref[...] = jnp.dot(s, w_ref[...], preferred_element_type=F32, precision=HIGHEST) + b_ref[...]


def _modulation(conds, w_mod, b_mod):
    n, d = conds.shape
    cols = w_mod.shape[1]
    tn = 512
    return pl.pallas_call(
        _mod_kernel,
        grid=(cols // tn,),
        in_specs=[pl.BlockSpec((n, d), lambda j: (0, 0)),
                  pl.BlockSpec((d, tn), lambda j: (0, j)),
                  pl.BlockSpec((1, tn), lambda j: (0, j))],
        out_specs=pl.BlockSpec((n, tn), lambda j: (0, j)),
        out_shape=jax.ShapeDtypeStruct((n, cols), F32),
        name="mod",
    )(conds, w_mod, b_mod.reshape(1, cols))


def _repack_kernel(tbl_ref, w_ref, o_ref):
    del tbl_ref
    o_ref[...] = w_ref[...].T.astype(BF16)


def _repack_weights(w_t, src_rows, tn):
    _, d = w_t.shape
    nblk = len(src_rows)
    grid_spec = pltpu.PrefetchScalarGridSpec(
        num_scalar_prefetch=1,
        grid=(nblk,),
        in_specs=[pl.BlockSpec((pl.Element(tn), pl.Element(d)),
                               lambda j, tbl: (pl.multiple_of(tbl[j], F32_SUBLANES), 0))],
        out_specs=pl.BlockSpec((d, tn), lambda j, tbl: (0, j)),
    )
    return pl.pallas_call(
        _repack_kernel,
        grid_spec=grid_spec,
        out_shape=jax.ShapeDtypeStruct((d, nblk * tn), BF16),
        name="repack",
    )(jnp.asarray(src_rows, jnp.int32), w_t)


def _repack_logits_kernel(w_ref, o_ref, *, n_cols):
    wt = w_ref[...].T
    lane = lax.broadcasted_iota(jnp.int32, wt.shape, 1)
    hi, lo = _split_bf16(jnp.where(lane < n_cols, wt, 0.0), 2)
    o_ref[:, 0:LANES] = hi
    o_ref[:, LANES:2 * LANES] = lo


def _repack_logit_weights(w_t, row0, n_cols):
    _, d = w_t.shape
    return pl.pallas_call(
        functools.partial(_repack_logits_kernel, n_cols=n_cols),
        grid=(1,),
        in_specs=[pl.BlockSpec((pl.Element(LANES), pl.Element(d)), lambda j: (row0, 0))],
        out_specs=pl.BlockSpec((d, 2 * LANES), lambda j: (0, 0)),
        out_shape=jax.ShapeDtypeStruct((d, 2 * LANES), BF16),
        name="repack_logits",
    )(w_t)


def _proj_kernel(x_ref, xp_ref, xn_ref, mod_ref, gpre_ref, w_ref, wab_ref, wconv_ref, y_ref, ab_ref, *,
                 kinds, col0, tiles_per_seq):
    tm = x_ref.shape[0]
    halo = xp_ref.shape[0]
    pos = lax.rem(pl.program_id(0), tiles_per_seq)
    xa = jnp.concatenate([xp_ref[...], x_ref[...], xn_ref[...]], axis=0)
    h = _prenorm(xa, gpre_ref[...], mod_ref[...])
    hb = h.astype(BF16)
    rows = lax.broadcasted_iota(jnp.int32, (tm, 1), 0)
    keep_dn = (rows != 0) | (pos != 0)
    keep_up = (rows != tm - 1) | (pos != tiles_per_seq - 1)
    for n, kind in enumerate(kinds):
        sl = slice(n * D_MODEL, (n + 1) * D_MODEL)
        wsl = slice(col0 + n * D_MODEL, col0 + (n + 1) * D_MODEL)
        p = jnp.dot(hb, w_ref[:, wsl], preferred_element_type=F32)
        wc = wconv_ref[:, wsl]
        p_dn = jnp.where(keep_dn, pltpu.roll(p, 1, axis=0)[halo:halo + tm], 0.0)
        p_up = jnp.where(keep_up, pltpu.roll(p, tm + 2 * halo - 1, axis=0)[halo:halo + tm], 0.0)
        y = _silu(p_dn * wc[0:1, :] + p[halo:halo + tm] * wc[1:2, :] + p_up * wc[2:3, :])
        if kind == "v":
            y_ref[:, sl] = y.astype(BF16)
            continue
        scale = HEAD_DIM ** -0.5 if kind == "q" else 1.0
        for hd in range(HEADS):
            yh = y[:, hd * HEAD_DIM:(hd + 1) * HEAD_DIM]
            inv = lax.rsqrt(jnp.sum(yh * yh, axis=-1, keepdims=True) + EPS) * scale
            y_ref[:, n * D_MODEL + hd * HEAD_DIM:n * D_MODEL + (hd + 1) * HEAD_DIM] = (yh * inv).astype(BF16)
    h_hi, h_lo = _split_bf16(h[halo:halo + tm], 2)
    hw = jnp.dot(h_hi, wab_ref[...], preferred_element_type=F32)
    ab_ref[...] = (hw[:, 0:LANES] + hw[:, LANES:2 * LANES]
                   + jnp.dot(h_lo, wab_ref[:, 0:LANES], preferred_element_type=F32))


def _project_qkv(x2d, mod3, tiles_per_mod, g_pre, w_pack, w_ab, w_conv, tm, seq, kinds):
    rows, d = x2d.shape
    nq = len(kinds) * d
    nw = 3 * d
    halo = F32_SUBLANES
    hb = tm // halo
    last_halo = rows // halo - 1
    kern = functools.partial(_proj_kernel, kinds=kinds, col0=nw - nq, tiles_per_seq=seq // tm)
    return pl.pallas_call(
        kern,
        grid=(rows // tm,),
        in_specs=[pl.BlockSpec((tm, d), lambda i: (i, 0)),
                  pl.BlockSpec((halo, d), lambda i: (jnp.maximum(i * hb - 1, 0), 0)),
                  pl.BlockSpec((halo, d), lambda i: (jnp.minimum((i + 1) * hb, last_halo), 0)),
                  pl.BlockSpec((None, 1, 3 * d), lambda i: (i // tiles_per_mod, 0, 0)),
                  pl.BlockSpec((1, d), lambda i: (0, 0)),
                  pl.BlockSpec((d, nw), lambda i: (0, 0)),
                  pl.BlockSpec((d, 2 * LANES), lambda i: (0, 0)),
                  pl.BlockSpec((3, nw), lambda i: (0, 0))],
        out_specs=[pl.BlockSpec((tm, nq), lambda i: (i, 0)),
                   pl.BlockSpec((tm, LANES), lambda i: (i, 0))],
        out_shape=[jax.ShapeDtypeStruct((rows, nq), BF16),
                   jax.ShapeDtypeStruct((rows, LANES), F32)],
        compiler_params=pltpu.CompilerParams(vmem_limit_bytes=VMEM_LIMIT),
        name="proj",
    )(x2d, x2d, x2d, mod3, g_pre, w_pack, w_ab, w_conv)


def _unit_tri_inverse(a_list, eye, ii, jj):
    c = a_list[0].shape[0]
    shift = BASE_BLOCK.bit_length() - 1
    same = (ii >> shift) == (jj >> shift)
    d1f = [jnp.where(same, a, 0.0) for a in a_list]
    d1 = [d.astype(BF16) for d in d1f]
    d2 = [jnp.dot(d, d, preferred_element_type=F32) for d in d1]
    d2b = [d.astype(BF16) for d in d2]
    d34 = [jnp.dot(jnp.concatenate([x, y], axis=0), y, preferred_element_type=F32)
           for x, y in zip(d1, d2b)]
    z = [eye - x + y - p[0:c] for x, y, p in zip(d1f, d2, d34)]
    t = [x + _bdot(x, p[c:2 * c]) for x, p in zip(z, d34)]
    b = BASE_BLOCK
    while b < c:
        s1 = b.bit_length() - 1
        off = ((ii >> (s1 + 1)) == (jj >> (s1 + 1))) & ((ii >> s1) != (jj >> s1))
        tb = [x.astype(BF16) for x in t]
        tc = [jnp.dot(x, jnp.where(off, a, 0.0).astype(BF16), preferred_element_type=F32)
              for x, a in zip(tb, a_list)]
        t = [x - jnp.dot(y.astype(BF16), xb, preferred_element_type=F32) for x, y, xb in zip(t, tc, tb)]
        b *= 2
    return t


def _gdn_kernel(*refs, reverse, with_q, accumulate, n_t, tg):
    it = iter(refs)
    y_ref, ab_ref, arow_ref, dtrow_ref, s0_ref = next(it), next(it), next(it), next(it), next(it)
    oin_ref = next(it) if accumulate else None
    o_ref = next(it) if with_q else None
    sfin_ref = None if with_q else next(it)
    s_scr = next(it)

    c = GDN_CHUNK
    i = pl.program_id(1)
    ncol = y_ref.shape[1]
    koff = ncol - 2 * D_MODEL
    voff = ncol - D_MODEL

    @pl.when(i == 0)
    def _():
        s_scr[...] = s0_ref[...]

    ab = ab_ref[...]
    g_cmp = -jnp.exp(arow_ref[...]) * _softplus(ab + dtrow_ref[...])
    beta_cmp = _sigmoid(ab)
    ti = lax.broadcasted_iota(jnp.int32, (tg, tg), 0)
    tj = lax.broadcasted_iota(jnp.int32, (tg, tg), 1)
    cs = c.bit_length() - 1
    same_chunk = (ti >> cs) == (tj >> cs)
    order = (tj >= ti) if reverse else (tj <= ti)
    cum_mat = jnp.where(same_chunk & order, 1.0, 0.0).astype(BF16)
    gcum3 = jnp.dot(cum_mat, jnp.concatenate(_split_bf16(g_cmp, 3), axis=1), preferred_element_type=F32)
    gcum_cmp = gcum3[:, 0:LANES] + gcum3[:, LANES:2 * LANES] + gcum3[:, 2 * LANES:3 * LANES]
    gcum_t = gcum_cmp.T
    eg_cmp = jnp.exp(gcum_cmp)
    dir_off = HEADS if reverse else 0

    ii = lax.broadcasted_iota(jnp.int32, (c, c), 0)
    jj = lax.broadcasted_iota(jnp.int32, (c, c), 1)
    eye = jnp.where(ii == jj, 1.0, 0.0).astype(F32)
    incl = (ii <= jj) if reverse else (ii >= jj)
    strict = (ii < jj) if reverse else (ii > jj)

    n_chunks = tg // c
    chunk_order = list(range(n_chunks - 1, -1, -1) if reverse else range(n_chunks))
    inst = [(cidx, h) for cidx in chunk_order for h in range(HEADS)]

    knb, qsb, be, gc, eg, decay = [], [], [], [], [], []
    for cidx, h in inst:
        rs = slice(cidx * c, (cidx + 1) * c)
        knb.append(y_ref[rs, koff + h * HEAD_DIM:koff + (h + 1) * HEAD_DIM])
        if with_q:
            qsb.append(y_ref[rs, h * HEAD_DIM:(h + 1) * HEAD_DIM])
        la = dir_off + h
        lb = 2 * HEADS + dir_off + h
        be.append(jnp.broadcast_to(beta_cmp[rs, lb:lb + 1], (c, HEAD_DIM)))
        g_col = jnp.broadcast_to(gcum_cmp[rs, la:la + 1], (c, HEAD_DIM))
        g_row = gcum_t[la:la + 1, cidx * c:(cidx + 1) * c]
        gc.append(g_col)
        decay.append(jnp.exp(jnp.where(incl, g_col - g_row, NEG_BIG)))
        eg.append(jnp.broadcast_to(eg_cmp[rs, la:la + 1], (c, HEAD_DIM)))
    kn = [k.astype(F32) for k in knb]
    kb = [k * b for k, b in zip(kn, be)]
    nt_dims = (((1,), (1,)), ((), ()))
    if with_q:
        qs = [q.astype(F32) for q in qsb]
        kk = [lax.dot_general(jnp.concatenate([a.astype(BF16), q], axis=0), k, nt_dims,
                              preferred_element_type=F32) for a, q, k in zip(kb, qsb, knb)]
        attn = [(x[c:2 * c] * d).astype(BF16) for x, d in zip(kk, decay)]
    else:
        kk = [lax.dot_general(a.astype(BF16), k, nt_dims, preferred_element_type=F32)
              for a, k in zip(kb, knb)]
    a_mat = [jnp.where(strict, x[0:c] * d, 0.0) for x, d in zip(kk, decay)]
    t_inv = _unit_tri_inverse(a_mat, eye, ii, jj)
    sol = []
    for n, (cidx, h) in enumerate(inst):
        rs = slice(cidx * c, (cidx + 1) * c)
        vv = y_ref[rs, voff + h * HEAD_DIM:voff + (h + 1) * HEAD_DIM].astype(F32)
        rhs = jnp.concatenate([vv * be[n], kb[n] * eg[n]], axis=1)
        sol.append(_bdot(t_inv[n], rhs))

    for ci in range(n_chunks):
        base = ci * HEADS
        idx = range(base, base + HEADS)
        rs = slice(chunk_order[ci] * c, (chunk_order[ci] + 1) * c)
        s_old = [s_scr[h] for h in range(HEADS)]
        s_b = [s.astype(BF16) for s in s_old]
        if with_q:
            ws = [jnp.dot(jnp.concatenate([sol[n][:, HEAD_DIM:], qs[n] * eg[n]], axis=0).astype(BF16), sb,
                          preferred_element_type=F32) for n, sb in zip(idx, s_b)]
        else:
            ws = [jnp.dot(sol[n][:, HEAD_DIM:].astype(BF16), sb, preferred_element_type=F32)
                  for n, sb in zip(idx, s_b)]
        v_new = [(sol[n][:, 0:HEAD_DIM] - w[0:c]).astype(BF16) for n, w in zip(idx, ws)]
        last = rs.start if reverse else rs.stop - 1
        gl_cmp = gcum_cmp[last:last + 1, :]
        tail_cmp = jnp.exp(gl_cmp - gcum_cmp[rs, :])
        egl_cmp = jnp.exp(gl_cmp)
        k_tail = [(kn[n] * jnp.broadcast_to(tail_cmp[:, dir_off + h:dir_off + h + 1], (c, HEAD_DIM))).T
                  .astype(BF16) for h, n in enumerate(idx)]
        if with_q:
            kva = [jnp.dot(jnp.concatenate([kt, attn[n]], axis=0), v, preferred_element_type=F32)
                   for kt, n, v in zip(k_tail, idx, v_new)]
        else:
            kva = [jnp.dot(kt, v, preferred_element_type=F32) for kt, v in zip(k_tail, v_new)]
        for h in range(HEADS):
            s_decay = jnp.broadcast_to(egl_cmp[:, dir_off + h:dir_off + h + 1], (HEAD_DIM, HEAD_DIM))
            s_scr[h] = s_old[h] * s_decay + kva[h][0:HEAD_DIM]
        if with_q:
            for h in range(HEADS):
                hs = slice(h * HEAD_DIM, (h + 1) * HEAD_DIM)
                o = ws[h][c:2 * c] + kva[h][HEAD_DIM:HEAD_DIM + c]
                if accumulate:
                    o = o + oin_ref[rs, hs]
                o_ref[rs, hs] = o

    if not with_q:
        @pl.when(i == n_t - 1)
        def _():
            sfin_ref[...] = s_scr[...]


def _gdn_scan(y, ab, a_row, dt_row, s0, o_in, *, batch, reverse, with_q, tg):
    rows, ncol = y.shape
    n_t = rows // batch // tg
    accumulate = o_in is not None

    def tile(b, i):
        return (b * n_t + ((n_t - 1 - i) if reverse else i), 0)

    state = lambda b, i: (b, 0, 0, 0)
    const = lambda b, i: (0, 0)
    in_specs = [
        pl.BlockSpec((tg, ncol), tile),
        pl.BlockSpec((tg, LANES), tile),
        pl.BlockSpec((1, LANES), const),
        pl.BlockSpec((1, LANES), const),
        pl.BlockSpec((None, HEADS, HEAD_DIM, HEAD_DIM), state),
    ]
    args = [y, ab, a_row, dt_row, s0]
    if accumulate:
        in_specs.append(pl.BlockSpec((tg, D_MODEL), tile))
        args.append(o_in)
    if with_q:
        out_specs = pl.BlockSpec((tg, D_MODEL), tile)
        out_shape = jax.ShapeDtypeStruct((rows, D_MODEL), F32)
    else:
        out_specs = pl.BlockSpec((None, HEADS, HEAD_DIM, HEAD_DIM), state)
        out_shape = jax.ShapeDtypeStruct((batch, HEADS, HEAD_DIM, HEAD_DIM), F32)
    kern = functools.partial(_gdn_kernel, reverse=reverse, with_q=with_q, accumulate=accumulate,
                             n_t=n_t, tg=tg)
    return pl.pallas_call(
        kern,
        grid=(batch, n_t),
        in_specs=in_specs,
        out_specs=out_specs,
        out_shape=out_shape,
        scratch_shapes=[pltpu.VMEM((HEADS, HEAD_DIM, HEAD_DIM), F32)],
        compiler_params=pltpu.CompilerParams(dimension_semantics=("arbitrary", "arbitrary"),
                                             vmem_limit_bytes=VMEM_LIMIT),
        name="gdn_bwd" if reverse else "gdn_fwd",
    )(*args)


def _brancha_kernel(x_ref, mod_ref, gpre_ref, wa_ref, lng_ref, lnb_ref, wsp_ref, bsp_ref, wpa_ref, ma_ref):
    d = D_MODEL
    tm = x_ref.shape[0]
    hb = _prenorm(x_ref[...], gpre_ref[...], mod_ref[...]).astype(BF16)
    vg = _gelu_tanh(jnp.dot(hb, wa_ref[:, d:2 * d], preferred_element_type=F32))
    mu = jnp.mean(vg, axis=-1, keepdims=True)
    vc = vg - mu
    var = jnp.mean(vc * vc, axis=-1, keepdims=True)
    v = (vc * lax.rsqrt(var + EPS) * lng_ref[...] + lnb_ref[...]).astype(BF16)
    parts = []
    for n in range(tm // GM_CHUNK):
        rs = slice(n * GM_CHUNK, (n + 1) * GM_CHUNK)
        cols = [jnp.dot(wsp_ref[g], v[rs, g * HEAD_DIM:(g + 1) * HEAD_DIM], preferred_element_type=F32)
                for g in range(d // HEAD_DIM)]
        parts.append(jnp.concatenate(cols, axis=1) + bsp_ref[...])
    s = jnp.concatenate(parts, axis=0)
    u = _gelu_tanh(jnp.dot(hb, wa_ref[:, 0:d], preferred_element_type=F32))
    z = _silu(jnp.dot(hb, wa_ref[:, 2 * d:3 * d], preferred_element_type=F32))
    ya = (u * s * z).astype(BF16)
    gate = _sigmoid(jnp.dot(hb, wa_ref[:, 3 * d:4 * d], preferred_element_type=F32))
    ma_ref[...] = gate * jnp.dot(ya, wpa_ref[...], preferred_element_type=F32)


def _branch_a(x2d, mod3, tiles_per_mod, g_pre, w_pack, ln_g, ln_b, w_sp, b_sp_e, w_pa, tm):
    rows, d = x2d.shape
    const = lambda i: (0, 0)
    return pl.pallas_call(
        _brancha_kernel,
        grid=(rows // tm,),
        in_specs=[pl.BlockSpec((tm, d), lambda i: (i, 0)),
                  pl.BlockSpec((None, 1, 3 * d), lambda i: (i // tiles_per_mod, 0, 0)),
                  pl.BlockSpec((1, d), const),
                  pl.BlockSpec((d, 4 * d), lambda i: (0, 1)),
                  pl.BlockSpec((1, d), const),
                  pl.BlockSpec((1, d), const),
                  pl.BlockSpec(w_sp.shape, lambda i: (0, 0, 0)),
                  pl.BlockSpec((GM_CHUNK, d), const),
                  pl.BlockSpec((d, d), const)],
        out_specs=pl.BlockSpec((tm, d), lambda i: (i, 0)),
        out_shape=jax.ShapeDtypeStruct((rows, d), F32),
        compiler_params=pltpu.CompilerParams(vmem_limit_bytes=VMEM_LIMIT),
        name="brancha",
    )(x2d, mod3, g_pre, w_pack, ln_g, ln_b, w_sp, b_sp_e, w_pa)


def _out_kernel(x_ref, mod_ref, gpre_ref, gpost_ref, o_ref, ma_ref, wzb_ref, wgb_ref, wpb_ref, wout_ref,
                gon_ref, out_ref):
    d = D_MODEL
    x = x_ref[...]
    mod = mod_ref[...]
    hb = _prenorm(x, gpre_ref[...], mod).astype(BF16)
    zb = _silu(jnp.dot(hb, wzb_ref[...], preferred_element_type=F32))
    gon = gon_ref[...]
    parts = []
    for h in range(HEADS):
        oh = o_ref[:, h * HEAD_DIM:(h + 1) * HEAD_DIM]
        ms = jnp.mean(oh * oh, axis=-1, keepdims=True)
        parts.append(oh * lax.rsqrt(ms + EPS) * gon)
    yb = (jnp.concatenate(parts, axis=1) * zb).astype(BF16)
    gate_b = _sigmoid(jnp.dot(hb, wgb_ref[...], preferred_element_type=F32))
    merged = ma_ref[...] + gate_b * jnp.dot(yb, wpb_ref[...], preferred_element_type=F32)
    z = jnp.dot(merged.astype(BF16), wout_ref[...], preferred_element_type=F32)
    zn = z * lax.rsqrt(jnp.mean(z * z, axis=-1, keepdims=True) + EPS) * gpost_ref[...]
    out_ref[...] = x + mod[:, 2 * d:3 * d] * zn


def _merge_out(x2d, mod3, tiles_per_mod, g_pre, g_post, o, m_a, w_pack, w_pb, w_out, g_onorm, tm):
    rows, d = x2d.shape
    const = lambda i: (0, 0)
    row = lambda i: (i, 0)
    return pl.pallas_call(
        _out_kernel,
        grid=(rows // tm,),
        in_specs=[pl.BlockSpec((tm, d), row),
                  pl.BlockSpec((None, 1, 3 * d), lambda i: (i // tiles_per_mod, 0, 0)),
                  pl.BlockSpec((1, d), const),
                  pl.BlockSpec((1, d), const),
                  pl.BlockSpec((tm, d), row),
                  pl.BlockSpec((tm, d), row),
                  pl.BlockSpec((d, d), lambda i: (0, 3)),
                  pl.BlockSpec((d, d), lambda i: (0, 8)),
                  pl.BlockSpec((d, d), const),
                  pl.BlockSpec((d, d), const),
                  pl.BlockSpec((1, HEAD_DIM), const)],
        out_specs=pl.BlockSpec((tm, d), row),
        out_shape=jax.ShapeDtypeStruct((rows, d), F32),
        compiler_params=pltpu.CompilerParams(vmem_limit_bytes=VMEM_LIMIT),
        name="merge_out",
    )(x2d, mod3, g_pre, g_post, o, m_a, w_pack, w_pack, w_pb, w_out, g_onorm)


def _layer(x, ctx, mod_x, mod_c, g_pre, g_post, w_in, w_conv, a_log, dt_bias, g_onorm, gm_ln_g, gm_ln_b,
           w_sp, b_sp, w_pa, w_pb, w_out):
    batch, seq, d = x.shape
    ctx_len = ctx.shape[1]
    kd = HEADS * HEAD_DIM
    off_a = 3 * kd
    off_zb = off_a + 4 * HEADS

    tn = 512
    w_t = jnp.transpose(w_in)
    w_pack = _repack_weights(w_t, list(range(0, off_a, tn)) + list(range(off_zb, w_in.shape[1], tn)), tn)
    w_ab2 = _repack_logit_weights(w_t, off_a, off_zb - off_a)
    a_row = jnp.pad(a_log.reshape(1, 2 * HEADS), ((0, 0), (0, LANES - 2 * HEADS)))
    dt_row = jnp.pad(dt_bias.reshape(1, 2 * HEADS), ((0, 0), (0, LANES - 2 * HEADS)))
    b_sp_e = jnp.repeat(b_sp.T, HEAD_DIM, axis=1)
    g_pre2 = g_pre.reshape(1, d)

    x2d = x.reshape(batch * seq, d)
    c2d = ctx.reshape(batch * ctx_len, d)
    tm = 512
    tm2 = 2 * tm

    kv_c, ab_c = _project_qkv(c2d, mod_c, 1 << 30, g_pre2, w_pack, w_ab2, w_conv, ctx_len, ctx_len, ("k", "v"))
    s_zero = jnp.zeros((batch, HEADS, HEAD_DIM, HEAD_DIM), F32)
    s_fwd = _gdn_scan(kv_c, ab_c, a_row, dt_row, s_zero, None,
                      batch=batch, reverse=False, with_q=False, tg=ctx_len)
    s_bwd = _gdn_scan(kv_c, ab_c, a_row, dt_row, s_zero, None,
                      batch=batch, reverse=True, with_q=False, tg=ctx_len)

    qkv, ab = _project_qkv(x2d, mod_x, seq // tm, g_pre2, w_pack, w_ab2, w_conv, tm, seq, ("q", "k", "v"))
    tg = 2 * GDN_CHUNK
    o_f = _gdn_scan(qkv, ab, a_row, dt_row, s_fwd, None, batch=batch, reverse=False, with_q=True, tg=tg)
    o = _gdn_scan(qkv, ab, a_row, dt_row, s_bwd, o_f, batch=batch, reverse=True, with_q=True, tg=tg)
    m_a = _branch_a(x2d, mod_x, seq // tm2, g_pre2, w_pack, gm_ln_g.reshape(1, d), gm_ln_b.reshape(1, d),
                    w_sp.astype(BF16), b_sp_e, w_pa.astype(BF16), tm2)
    out = _merge_out(x2d, mod_x, seq // tm2, g_pre2, g_post.reshape(1, d), o, m_a, w_pack,
                     w_pb.astype(BF16), w_out.astype(BF16), g_onorm.reshape(1, HEAD_DIM), tm2)
    return out.reshape(batch, seq, d)


def kernel(x, c, ctx, c_ctx, w_mod, b_mod, g_pre, g_post, w_in, w_conv, a_log, dt_bias, g_onorm, gm_ln_g,
           gm_ln_b, w_sp, b_sp, w_pa, w_pb, w_out):
    batch, _, d = x.shape
    depth = w_mod.shape[0]
    assert depth == 1, "context tokens are only read (never updated) by a single-layer stack"
    assert ctx.shape[1] % GDN_CHUNK == 0 and x.shape[1] % (8 * GDN_CHUNK) == 0 and d == D_MODEL
    pad = (-(batch + 1)) % 8
    conds = jnp.concatenate([c, c_ctx[None, :], jnp.zeros((pad, d), F32)], axis=0)
    i = 0
    m = _modulation(conds, w_mod[i], b_mod[i])
    mod_x = m[0:batch].reshape(batch, 1, 3 * d)
    mod_c = m[batch:batch + 1].reshape(1, 1, 3 * d)
    return _layer(x, ctx, mod_x, mod_c, g_pre[i], g_post[i], w_in[i], w_conv[i], a_log[i], dt_bias[i],
                  g_onorm[i], gm_ln_g[i], gm_ln_b[i], w_sp[i], b_sp[i], w_pa[i], w_pb[i], w_out[i])
```

```python
import functools

import jax
import jax.numpy as jnp
from jax import lax
from jax.experimental import pallas as pl
from jax.experimental.pallas import tpu as pltpu

F32 = jnp.float32
BF16 = jnp.bfloat16
HIGHEST = lax.Precision.HIGHEST

D_MODEL = 1024
HEADS = 8
HEAD_DIM = 128
GM_CHUNK = 128
EPS = 1e-6
NEG_BIG = -1e30

LANES = 128
F32_SUBLANES = 8
GDN_CHUNK = 128
BASE_BLOCK = 8
VMEM_LIMIT = 56 * 1024 * 1024


def _sigmoid(x):
    return 1.0 / (1.0 + jnp.exp(-x))


def _silu(x):
    return x * _sigmoid(x)


def _gelu_tanh(x):
    c = 0.7978845608028654
    return 0.5 * x * (1.0 + jnp.tanh(c * (x + 0.044715 * (x * x * x))))


def _softplus(x):
    return jnp.maximum(x, 0.0) + jnp.log1p(jnp.exp(-jnp.abs(x)))


def _prenorm(x, g_pre, mod):
    ms = jnp.mean(x * x, axis=-1, keepdims=True)
    y = x * lax.rsqrt(ms + EPS) * g_pre
    return y * (1.0 + mod[:, D_MODEL:2 * D_MODEL]) + mod[:, 0:D_MODEL]


def _bdot(a, b):
    return jnp.dot(a.astype(BF16), b.astype(BF16), preferred_element_type=F32)


def _split_bf16(x, parts):
    out, r = [], x
    for _ in range(parts):
        p = r.astype(BF16)
        out.append(p)
        r = r - p.astype(F32)
    return out


def _mod_kernel(c_ref, w_ref, b_ref, o_ref):
    s = _silu(c_ref[...])
    o_ref[...] = jnp.dot(s, w_ref[...], preferred_element_type=F32, precision=HIGHEST) + b_ref[...]


def _modulation(conds, w_mod, b_mod):
    n, d = conds.shape
    cols = w_mod.shape[1]
    tn = 1024
    return pl.pallas_call(
        _mod_kernel,
        grid=(cols // tn,),
        in_specs=[pl.BlockSpec((n, d), lambda j: (0, 0)),
                  pl.BlockSpec((d, tn), lambda j: (0, j)),
                  pl.BlockSpec((1, tn), lambda j: (0, j))],
        out_specs=pl.BlockSpec((n, tn), lambda j: (0, j)),
        out_shape=jax.ShapeDtypeStruct((n, cols), F32),
        name="mod",
    )(conds, w_mod, b_mod.reshape(1, cols))


def _repack_kernel(tbl_ref, w_ref, o_ref):
    del tbl_ref
    o_ref[...] = w_ref[...].T.astype(BF16)


def _repack_weights(w_t, src_rows, tn):
    _, d = w_t.shape
    nblk = len(src_rows)
    grid_spec = pltpu.PrefetchScalarGridSpec(
        num_scalar_prefetch=1,
        grid=(nblk,),
        in_specs=[pl.BlockSpec((pl.Element(tn), pl.Element(d)),
                               lambda j, tbl: (pl.multiple_of(tbl[j], F32_SUBLANES), 0))],
        out_specs=pl.BlockSpec((d, tn), lambda j, tbl: (0, j)),
    )
    return pl.pallas_call(
        _repack_kernel,
        grid_spec=grid_spec,
        out_shape=jax.ShapeDtypeStruct((d, nblk * tn), BF16),
        name="repack",
    )(jnp.asarray(src_rows, jnp.int32), w_t)


def _repack_logits_kernel(w_ref, o_ref, *, n_cols):
    wt = w_ref[...].T
    lane = lax.broadcasted_iota(jnp.int32, wt.shape, 1)
    hi, lo = _split_bf16(jnp.where(lane < n_cols, wt, 0.0), 2)
    o_ref[:, 0:LANES] = hi
    o_ref[:, LANES:2 * LANES] = lo


def _repack_logit_weights(w_t, row0, n_cols):
    _, d = w_t.shape
    return pl.pallas_call(
        functools.partial(_repack_logits_kernel, n_cols=n_cols),
        grid=(1,),
        in_specs=[pl.BlockSpec((pl.Element(LANES), pl.Element(d)), lambda j: (row0, 0))],
        out_specs=pl.BlockSpec((d, 2 * LANES), lambda j: (0, 0)),
        out_shape=jax.ShapeDtypeStruct((d, 2 * LANES), BF16),
        name="repack_logits",
    )(w_t)


def _proj_kernel(x_ref, xp_ref, xn_ref, mod_ref, gpre_ref, w_ref, wab_ref, wconv_ref, y_ref, ab_ref, *,
                 kinds, col0, tiles_per_seq):
    tm = x_ref.shape[0]
    halo = xp_ref.shape[0]
    pos = lax.rem(pl.program_id(0), tiles_per_seq)
    xa = jnp.concatenate([xp_ref[...], x_ref[...], xn_ref[...]], axis=0)
    h = _prenorm(xa, gpre_ref[...], mod_ref[...])
    hb = h.astype(BF16)
    rows = lax.broadcasted_iota(jnp.int32, (tm, 1), 0)
    keep_dn = (rows != 0) | (pos != 0)
    keep_up = (rows != tm - 1) | (pos != tiles_per_seq - 1)
    for n, kind in enumerate(kinds):
        sl = slice(n * D_MODEL, (n + 1) * D_MODEL)
        wsl = slice(col0 + n * D_MODEL, col0 + (n + 1) * D_MODEL)
        p = jnp.dot(hb, w_ref[:, wsl], preferred_element_type=F32)
        wc = wconv_ref[:, wsl]
        p_dn = jnp.where(keep_dn, pltpu.roll(p, 1, axis=0)[halo:halo + tm], 0.0)
        p_up = jnp.where(keep_up, pltpu.roll(p, tm + 2 * halo - 1, axis=0)[halo:halo + tm], 0.0)
        y = _silu(p_dn * wc[0:1, :] + p[halo:halo + tm] * wc[1:2, :] + p_up * wc[2:3, :])
        if kind == "v":
            y_ref[:, sl] = y.astype(BF16)
            continue
        scale = HEAD_DIM ** -0.5 if kind == "q" else 1.0
        for hd in range(HEADS):
            yh = y[:, hd * HEAD_DIM:(hd + 1) * HEAD_DIM]
            inv = lax.rsqrt(jnp.sum(yh * yh, axis=-1, keepdims=True) + EPS) * scale
            y_ref[:, n * D_MODEL + hd * HEAD_DIM:n * D_MODEL + (hd + 1) * HEAD_DIM] = (yh * inv).astype(BF16)
    h_hi, h_lo = _split_bf16(h[halo:halo + tm], 2)
    hw = jnp.dot(h_hi, wab_ref[...], preferred_element_type=F32)
    ab_ref[...] = (hw[:, 0:LANES] + hw[:, LANES:2 * LANES]
                   + jnp.dot(h_lo, wab_ref[:, 0:LANES], preferred_element_type=F32))


def _project_qkv(x2d, mod3, tiles_per_mod, g_pre, w_pack, w_ab, w_conv, tm, seq, kinds):
    rows, d = x2d.shape
    nq = len(kinds) * d
    nw = 3 * d
    halo = F32_SUBLANES
    hb = tm // halo
    last_halo = rows // halo - 1
    kern = functools.partial(_proj_kernel, kinds=kinds, col0=nw - nq, tiles_per_seq=seq // tm)
    return pl.pallas_call(
        kern,
        grid=(rows // tm,),
        in_specs=[pl.BlockSpec((tm, d), lambda i: (i, 0)),
                  pl.BlockSpec((halo, d), lambda i: (jnp.maximum(i * hb - 1, 0), 0)),
                  pl.BlockSpec((halo, d), lambda i: (jnp.minimum((i + 1) * hb, last_halo), 0)),
                  pl.BlockSpec((None, 1, 3 * d), lambda i: (i // tiles_per_mod, 0, 0)),
                  pl.BlockSpec((1, d), lambda i: (0, 0)),
                  pl.BlockSpec((d, nw), lambda i: (0, 0)),
                  pl.BlockSpec((d, 2 * LANES), lambda i: (0, 0)),
                  pl.BlockSpec((3, nw), lambda i: (0, 0))],
        out_specs=[pl.BlockSpec((tm, nq), lambda i: (i, 0)),
                   pl.BlockSpec((tm, LANES), lambda i: (i, 0))],
        out_shape=[jax.ShapeDtypeStruct((rows, nq), BF16),
                   jax.ShapeDtypeStruct((rows, LANES), F32)],
        compiler_params=pltpu.CompilerParams(vmem_limit_bytes=VMEM_LIMIT),
        name="proj",
    )(x2d, x2d, x2d, mod3, g_pre, w_pack, w_ab, w_conv)


def _unit_tri_inverse(a_list, eye, ii, jj):
    c = a_list[0].shape[0]
    shift = BASE_BLOCK.bit_length() - 1
    same = (ii >> shift) == (jj >> shift)
    d1f = [jnp.where(same, a, 0.0) for a in a_list]
    d1 = [d.astype(BF16) for d in d1f]
    d2 = [jnp.dot(d, d, preferred_element_type=F32) for d in d1]
    d2b = [d.astype(BF16) for d in d2]
    d34 = [jnp.dot(jnp.concatenate([x, y], axis=0), y, preferred_element_type=F32)
           for x, y in zip(d1, d2b)]
    z = [eye - x + y - p[0:c] for x, y, p in zip(d1f, d2, d34)]
    t = [x + _bdot(x, p[c:2 * c]) for x, p in zip(z, d34)]
    b = BASE_BLOCK
    while b < c:
        s1 = b.bit_length() - 1
        off = ((ii >> (s1 + 1)) == (jj >> (s1 + 1))) & ((ii >> s1) != (jj >> s1))
        tb = [x.astype(BF16) for x in t]
        tc = [jnp.dot(x, jnp.where(off, a, 0.0).astype(BF16), preferred_element_type=F32)
              for x, a in zip(tb, a_list)]
        t = [x - jnp.dot(y.astype(BF16), xb, preferred_element_type=F32) for x, y, xb in zip(t, tc, tb)]
        b *= 2
    return t


def _gdn_kernel(*refs, reverse, with_q, accumulate, n_t, tg):
    it = iter(refs)
    y_ref, ab_ref, arow_ref, dtrow_ref, s0_ref = next(it), next(it), next(it), next(it), next(it)
    oin_ref = next(it) if accumulate else None
    o_ref = next(it) if with_q else None
    sfin_ref = None if with_q else next(it)
    s_scr = next(it)

    c = GDN_CHUNK
    i = pl.program_id(1)
    ncol = y_ref.shape[1]
    koff = ncol - 2 * D_MODEL
    voff = ncol - D_MODEL

    @pl.when(i == 0)
    def _():
        s_scr[...] = s0_ref[...]

    ab = ab_ref[...]
    g_cmp = -jnp.exp(arow_ref[...]) * _softplus(ab + dtrow_ref[...])
    beta_cmp = _sigmoid(ab)
    ti = lax.broadcasted_iota(jnp.int32, (tg, tg), 0)
    tj = lax.broadcasted_iota(jnp.int32, (tg, tg), 1)
    cs = c.bit_length() - 1
    same_chunk = (ti >> cs) == (tj >> cs)
    order = (tj >= ti) if reverse else (tj <= ti)
    cum_mat = jnp.where(same_chunk & order, 1.0, 0.0).astype(BF16)
    gcum3 = jnp.dot(cum_mat, jnp.concatenate(_split_bf16(g_cmp, 3), axis=1), preferred_element_type=F32)
    gcum_cmp = gcum3[:, 0:LANES] + gcum3[:, LANES:2 * LANES] + gcum3[:, 2 * LANES:3 * LANES]
    gcum_t = gcum_cmp.T
    eg_cmp = jnp.exp(gcum_cmp)
    dir_off = HEADS if reverse else 0

    ii = lax.broadcasted_iota(jnp.int32, (c, c), 0)
    jj = lax.broadcasted_iota(jnp.int32, (c, c), 1)
    eye = jnp.where(ii == jj, 1.0, 0.0).astype(F32)
    incl = (ii <= jj) if reverse else (ii >= jj)
    strict = (ii < jj) if reverse else (ii > jj)

    n_chunks = tg // c
    chunk_order = list(range(n_chunks - 1, -1, -1) if reverse else range(n_chunks))
    inst = [(cidx, h) for cidx in chunk_order for h in range(HEADS)]

    knb, qsb, be, gc, eg, decay = [], [], [], [], [], []
    for cidx, h in inst:
        rs = slice(cidx * c, (cidx + 1) * c)
        knb.append(y_ref[rs, koff + h * HEAD_DIM:koff + (h + 1) * HEAD_DIM])
        if with_q:
            qsb.append(y_ref[rs, h * HEAD_DIM:(h + 1) * HEAD_DIM])
        la = dir_off + h
        lb = 2 * HEADS + dir_off + h
        be.append(jnp.broadcast_to(beta_cmp[rs, lb:lb + 1], (c, HEAD_DIM)))
        g_col = jnp.broadcast_to(gcum_cmp[rs, la:la + 1], (c, HEAD_DIM))
        g_row = gcum_t[la:la + 1, cidx * c:(cidx + 1) * c]
        gc.append(g_col)
        decay.append(jnp.exp(jnp.where(incl, g_col - g_row, NEG_BIG)))
        eg.append(jnp.broadcast_to(eg_cmp[rs, la:la + 1], (c, HEAD_DIM)))
    kn = [k.astype(F32) for k in knb]
    kb = [k * b for k, b in zip(kn, be)]
    nt_dims = (((1,), (1,)), ((), ()))
    if with_q:
        qs = [q.astype(F32) for q in qsb]
        kk = [lax.dot_general(jnp.concatenate([a.astype(BF16), q], axis=0), k, nt_dims,
                              preferred_element_type=F32) for a, q, k in zip(kb, qsb, knb)]
        attn = [(x[c:2 * c] * d).astype(BF16) for x, d in zip(kk, decay)]
    else:
        kk = [lax.dot_general(a.astype(BF16), k, nt_dims, preferred_element_type=F32)
              for a, k in zip(kb, knb)]
    a_mat = [jnp.where(strict, x[0:c] * d, 0.0) for x, d in zip(kk, decay)]
    t_inv = _unit_tri_inverse(a_mat, eye, ii, jj)
    sol = []
    for n, (cidx, h) in enumerate(inst):
        rs = slice(cidx * c, (cidx + 1) * c)
        vv = y_ref[rs, voff + h * HEAD_DIM:voff + (h + 1) * HEAD_DIM].astype(F32)
        rhs = jnp.concatenate([vv * be[n], kb[n] * eg[n]], axis=1)
        sol.append(_bdot(t_inv[n], rhs))

    for ci in range(n_chunks):
        base = ci * HEADS
        idx = range(base, base + HEADS)
        rs = slice(chunk_order[ci] * c, (chunk_order[ci] + 1) * c)
        s_old = [s_scr[h] for h in range(HEADS)]
        s_b = [s.astype(BF16) for s in s_old]
        if with_q:
            ws = [jnp.dot(jnp.concatenate([sol[n][:, HEAD_DIM:], qs[n] * eg[n]], axis=0).astype(BF16), sb,
                          preferred_element_type=F32) for n, sb in zip(idx, s_b)]
        else:
            ws = [jnp.dot(sol[n][:, HEAD_DIM:].astype(BF16), sb, preferred_element_type=F32)
                  for n, sb in zip(idx, s_b)]
        v_new = [(sol[n][:, 0:HEAD_DIM] - w[0:c]).astype(BF16) for n, w in zip(idx, ws)]
        last = rs.start if reverse else rs.stop - 1
        gl_cmp = gcum_cmp[last:last + 1, :]
        tail_cmp = jnp.exp(gl_cmp - gcum_cmp[rs, :])
        egl_cmp = jnp.exp(gl_cmp)
        k_tail = [(kn[n] * jnp.broadcast_to(tail_cmp[:, dir_off + h:dir_off + h + 1], (c, HEAD_DIM))).T
                  .astype(BF16) for h, n in enumerate(idx)]
        if with_q:
            kva = [jnp.dot(jnp.concatenate([kt, attn[n]], axis=0), v, preferred_element_type=F32)
                   for kt, n, v in zip(k_tail, idx, v_new)]
        else:
            kva = [jnp.dot(kt, v, preferred_element_type=F32) for kt, v in zip(k_tail, v_new)]
        for h in range(HEADS):
            s_decay = jnp.broadcast_to(egl_cmp[:, dir_off + h:dir_off + h + 1], (HEAD_DIM, HEAD_DIM))
            s_scr[h] = s_old[h] * s_decay + kva[h][0:HEAD_DIM]
        if with_q:
            for h in range(HEADS):
                hs = slice(h * HEAD_DIM, (h + 1) * HEAD_DIM)
                o = ws[h][c:2 * c] + kva[h][HEAD_DIM:HEAD_DIM + c]
                if accumulate:
                    o = o + oin_ref[rs, hs]
                o_ref[rs, hs] = o

    if not with_q:
        @pl.when(i == n_t - 1)
        def _():
            sfin_ref[...] = s_scr[...]


def _gdn_scan(y, ab, a_row, dt_row, s0, o_in, *, batch, reverse, with_q, tg):
    rows, ncol = y.shape
    n_t = rows // batch // tg
    accumulate = o_in is not None

    def tile(b, i):
        return (b * n_t + ((n_t - 1 - i) if reverse else i), 0)

    state = lambda b, i: (b, 0, 0, 0)
    const = lambda b, i: (0, 0)
    in_specs = [
        pl.BlockSpec((tg, ncol), tile),
        pl.BlockSpec((tg, LANES), tile),
        pl.BlockSpec((1, LANES), const),
        pl.BlockSpec((1, LANES), const),
        pl.BlockSpec((None, HEADS, HEAD_DIM, HEAD_DIM), state),
    ]
    args = [y, ab, a_row, dt_row, s0]
    if accumulate:
        in_specs.append(pl.BlockSpec((tg, D_MODEL), tile))
        args.append(o_in)
    if with_q:
        out_specs = pl.BlockSpec((tg, D_MODEL), tile)
        out_shape = jax.ShapeDtypeStruct((rows, D_MODEL), F32)
    else:
        out_specs = pl.BlockSpec((None, HEADS, HEAD_DIM, HEAD_DIM), state)
        out_shape = jax.ShapeDtypeStruct((batch, HEADS, HEAD_DIM, HEAD_DIM), F32)
    kern = functools.partial(_gdn_kernel, reverse=reverse, with_q=with_q, accumulate=accumulate,
                             n_t=n_t, tg=tg)
    return pl.pallas_call(
        kern,
        grid=(batch, n_t),
        in_specs=in_specs,
        out_specs=out_specs,
        out_shape=out_shape,
        scratch_shapes=[pltpu.VMEM((HEADS, HEAD_DIM, HEAD_DIM), F32)],
        compiler_params=pltpu.CompilerParams(dimension_semantics=("arbitrary", "arbitrary"),
                                             vmem_limit_bytes=VMEM_LIMIT),
        name="gdn_bwd" if reverse else "gdn_fwd",
    )(*args)


def _brancha_kernel(x_ref, mod_ref, gpre_ref, wa_ref, lng_ref, lnb_ref, wsp_ref, bsp_ref, wpa_ref, ma_ref):
    d = D_MODEL
    tm = x_ref.shape[0]
    hb = _prenorm(x_ref[...], gpre_ref[...], mod_ref[...]).astype(BF16)
    vg = _gelu_tanh(jnp.dot(hb, wa_ref[:, d:2 * d], preferred_element_type=F32))
    mu = jnp.mean(vg, axis=-1, keepdims=True)
    vc = vg - mu
    var = jnp.mean(vc * vc, axis=-1, keepdims=True)
    v = (vc * lax.rsqrt(var + EPS) * lng_ref[...] + lnb_ref[...]).astype(BF16)
    parts = []
    for n in range(tm // GM_CHUNK):
        rs = slice(n * GM_CHUNK, (n + 1) * GM_CHUNK)
        cols = [jnp.dot(wsp_ref[g], v[rs, g * HEAD_DIM:(g + 1) * HEAD_DIM], preferred_element_type=F32)
                for g in range(d // HEAD_DIM)]
        parts.append(jnp.concatenate(cols, axis=1) + bsp_ref[...])
    s = jnp.concatenate(parts, axis=0)
    u = _gelu_tanh(jnp.dot(hb, wa_ref[:, 0:d], preferred_element_type=F32))
    z = _silu(jnp.dot(hb, wa_ref[:, 2 * d:3 * d], preferred_element_type=F32))
    ya = (u * s * z).astype(BF16)
    gate = _sigmoid(jnp.dot(hb, wa_ref[:, 3 * d:4 * d], preferred_element_type=F32))
    ma_ref[...] = gate * jnp.dot(ya, wpa_ref[...], preferred_element_type=F32)


def _branch_a(x2d, mod3, tiles_per_mod, g_pre, w_pack, ln_g, ln_b, w_sp, b_sp_e, w_pa, tm):
    rows, d = x2d.shape
    const = lambda i: (0, 0)
    return pl.pallas_call(
        _brancha_kernel,
        grid=(rows // tm,),
        in_specs=[pl.BlockSpec((tm, d), lambda i: (i, 0)),
                  pl.BlockSpec((None, 1, 3 * d), lambda i: (i // tiles_per_mod, 0, 0)),
                  pl.BlockSpec((1, d), const),
                  pl.BlockSpec((d, 4 * d), lambda i: (0, 1)),
                  pl.BlockSpec((1, d), const),
                  pl.BlockSpec((1, d), const),
                  pl.BlockSpec(w_sp.shape, lambda i: (0, 0, 0)),
                  pl.BlockSpec((GM_CHUNK, d), const),
                  pl.BlockSpec((d, d), const)],
        out_specs=pl.BlockSpec((tm, d), lambda i: (i, 0)),
        out_shape=jax.ShapeDtypeStruct((rows, d), F32),
        compiler_params=pltpu.CompilerParams(vmem_limit_bytes=VMEM_LIMIT),
        name="brancha",
    )(x2d, mod3, g_pre, w_pack, ln_g, ln_b, w_sp, b_sp_e, w_pa)


def _out_kernel(x_ref, mod_ref, gpre_ref, gpost_ref, o_ref, ma_ref, wzb_ref, wgb_ref, wpb_ref, wout_ref,
                gon_ref, out_ref):
    d = D_MODEL
    x = x_ref[...]
    mod = mod_ref[...]
    hb = _prenorm(x, gpre_ref[...], mod).astype(BF16)
    zb = _silu(jnp.dot(hb, wzb_ref[...], preferred_element_type=F32))
    gon = gon_ref[...]
    parts = []
    for h in range(HEADS):
        oh = o_ref[:, h * HEAD_DIM:(h + 1) * HEAD_DIM]
        ms = jnp.mean(oh * oh, axis=-1, keepdims=True)
        parts.append(oh * lax.rsqrt(ms + EPS) * gon)
    yb = (jnp.concatenate(parts, axis=1) * zb).astype(BF16)
    gate_b = _sigmoid(jnp.dot(hb, wgb_ref[...], preferred_element_type=F32))
    merged = ma_ref[...] + gate_b * jnp.dot(yb, wpb_ref[...], preferred_element_type=F32)
    z = jnp.dot(merged.astype(BF16), wout_ref[...], preferred_element_type=F32)
    zn = z * lax.rsqrt(jnp.mean(z * z, axis=-1, keepdims=True) + EPS) * gpost_ref[...]
    out_ref[...] = x + mod[:, 2 * d:3 * d] * zn


def _merge_out(x2d, mod3, tiles_per_mod, g_pre, g_post, o, m_a, w_pack, w_pb, w_out, g_onorm, tm):
    rows, d = x2d.shape
    const = lambda i: (0, 0)
    row = lambda i: (i, 0)
    return pl.pallas_call(
        _out_kernel,
        grid=(rows // tm,),
        in_specs=[pl.BlockSpec((tm, d), row),
                  pl.BlockSpec((None, 1, 3 * d), lambda i: (i // tiles_per_mod, 0, 0)),
                  pl.BlockSpec((1, d), const),
                  pl.BlockSpec((1, d), const),
                  pl.BlockSpec((tm, d), row),
                  pl.BlockSpec((tm, d), row),
                  pl.BlockSpec((d, d), lambda i: (0, 3)),
                  pl.BlockSpec((d, d), lambda i: (0, 8)),
                  pl.BlockSpec((d, d), const),
                  pl.BlockSpec((d, d), const),
                  pl.BlockSpec((1, HEAD_DIM), const)],
        out_specs=pl.BlockSpec((tm, d), row),
        out_shape=jax.ShapeDtypeStruct((rows, d), F32),
        compiler_params=pltpu.CompilerParams(vmem_limit_bytes=VMEM_LIMIT),
        name="merge_out",
    )(x2d, mod3, g_pre, g_post, o, m_a, w_pack, w_pack, w_pb, w_out, g_onorm)


def _layer(x, ctx, mod_x, mod_c, g_pre, g_post, w_in, w_conv, a_log, dt_bias, g_onorm, gm_ln_g, gm_ln_b,
           w_sp, b_sp, w_pa, w_pb, w_out):
    batch, seq, d = x.shape
    ctx_len = ctx.shape[1]
    kd = HEADS * HEAD_DIM
    off_a = 3 * kd
    off_zb = off_a + 4 * HEADS

    tn = 1024
    w_t = jnp.transpose(w_in)
    w_pack = _repack_weights(w_t, list(range(0, off_a, tn)) + list(range(off_zb, w_in.shape[1], tn)), tn)
    w_ab2 = _repack_logit_weights(w_t, off_a, off_zb - off_a)
    a_row = jnp.pad(a_log.reshape(1, 2 * HEADS), ((0, 0), (0, LANES - 2 * HEADS)))
    dt_row = jnp.pad(dt_bias.reshape(1, 2 * HEADS), ((0, 0), (0, LANES - 2 * HEADS)))
    b_sp_e = jnp.repeat(b_sp.T, HEAD_DIM, axis=1)
    g_pre2 = g_pre.reshape(1, d)

    x2d = x.reshape(batch * seq, d)
    c2d = ctx.reshape(batch * ctx_len, d)
    tm = 1024

    kv_c, ab_c = _project_qkv(c2d, mod_c, 1 << 30, g_pre2, w_pack, w_ab2, w_conv, ctx_len, ctx_len, ("k", "v"))
    s_zero = jnp.zeros((batch, HEADS, HEAD_DIM, HEAD_DIM), F32)
    s_fwd = _gdn_scan(kv_c, ab_c, a_row, dt_row, s_zero, None,
                      batch=batch, reverse=False, with_q=False, tg=ctx_len)
    s_bwd = _gdn_scan(kv_c, ab_c, a_row, dt_row, s_zero, None,
                      batch=batch, reverse=True, with_q=False, tg=ctx_len)

    qkv, ab = _project_qkv(x2d, mod_x, seq // tm, g_pre2, w_pack, w_ab2, w_conv, tm, seq, ("q", "k", "v"))
    tg = 2 * GDN_CHUNK
    o_f = _gdn_scan(qkv, ab, a_row, dt_row, s_fwd, None, batch=batch, reverse=False, with_q=True, tg=tg)
    o = _gdn_scan(qkv, ab, a_row, dt_row, s_bwd, o_f, batch=batch, reverse=True, with_q=True, tg=tg)
    m_a = _branch_a(x2d, mod_x, seq // tm, g_pre2, w_pack, gm_ln_g.reshape(1, d), gm_ln_b.reshape(1, d),
                    w_sp.astype(BF16), b_sp_e, w_pa.astype(BF16), tm)
    out = _merge_out(x2d, mod_x, seq // tm, g_pre2, g_post.reshape(1, d), o, m_a, w_pack,
                     w_pb.astype(BF16), w_out.astype(BF16), g_onorm.reshape(1, HEAD_DIM), tm)
    return out.reshape(batch, seq, d)


def kernel(x, c, ctx, c_ctx, w_mod, b_mod, g_pre, g_post, w_in, w_conv, a_log, dt_bias, g_onorm, gm_ln_g,
           gm_ln_b, w_sp, b_sp, w_pa, w_pb, w_out):
    batch, _, d = x.shape
    depth = w_mod.shape[0]
    assert depth == 1, "context tokens are only read (never updated) by a single-layer stack"
    assert ctx.shape[1] % GDN_CHUNK == 0 and x.shape[1] % (8 * GDN_CHUNK) == 0 and d == D_MODEL
    pad = (-(batch + 1)) % 8
    conds = jnp.concatenate([c, c_ctx[None, :], jnp.zeros((pad, d), F32)], axis=0)
    i = 0
    m = _modulation(conds, w_mod[i], b_mod[i])
    mod_x = m[0:batch].reshape(batch, 1, 3 * d)
    mod_c = m[batch:batch + 1].reshape(1, 1, 3 * d)
    return _layer(x, ctx, mod_x, mod_c, g_pre[i], g_post[i], w_in[i], w_conv[i], a_log[i], dt_bias[i],
                  g_onorm[i], gm_ln_g[i], gm_ln_b[i], w_sp[i], b_sp[i], w_pa[i], w_pb[i], w_out[i])
```

```python
import functools

import jax
import jax.numpy as jnp
from jax import lax
from jax.experimental import pallas as pl
from jax.experimental.pallas import tpu as pltpu

F32 = jnp.float32
BF16 = jnp.bfloat16
HIGHEST = lax.Precision.HIGHEST

D_MODEL = 1024
HEADS = 8
HEAD_DIM = 128
GM_CHUNK = 128
EPS = 1e-6
NEG_BIG = -1e30

LANES = 128
F32_SUBLANES = 8
GDN_CHUNK = 128
BASE_BLOCK = 8
VMEM_LIMIT = 56 * 1024 * 1024


def _sigmoid(x):
    return 1.0 / (1.0 + jnp.exp(-x))


def _silu(x):
    return x * _sigmoid(x)


def _gelu_tanh(x):
    c = 0.7978845608028654
    return 0.5 * x * (1.0 + jnp.tanh(c * (x + 0.044715 * (x * x * x))))


def _softplus(x):
    return jnp.maximum(x, 0.0) + jnp.log1p(jnp.exp(-jnp.abs(x)))


def _prenorm(x, g_pre, mod):
    ms = jnp.mean(x * x, axis=-1, keepdims=True)
    y = x * lax.rsqrt(ms + EPS) * g_pre
    return y * (1.0 + mod[:, D_MODEL:2 * D_MODEL]) + mod[:, 0:D_MODEL]


def _bdot(a, b):
    return jnp.dot(a.astype(BF16), b.astype(BF16), preferred_element_type=F32)


def _split_bf16(x, parts):
    out, r = [], x
    for _ in range(parts):
        p = r.astype(BF16)
        out.append(p)
        r = r - p.astype(F32)
    return out


def _mod_kernel(c_ref, w_ref, b_ref, o_ref):
    s = _silu(c_ref[...])
    o_ref[...] = jnp.dot(s, w_ref[...], preferred_element_type=F32, precision=HIGHEST) + b_ref[...]


def _modulation(conds, w_mod, b_mod):
    n, d = conds.shape
    cols = w_mod.shape[1]
    tn = 1024
    return pl.pallas_call(
        _mod_kernel,
        grid=(cols // tn,),
        in_specs=[pl.BlockSpec((n, d), lambda j: (0, 0)),
                  pl.BlockSpec((d, tn), lambda j: (0, j)),
                  pl.BlockSpec((1, tn), lambda j: (0, j))],
        out_specs=pl.BlockSpec((n, tn), lambda j: (0, j)),
        out_shape=jax.ShapeDtypeStruct((n, cols), F32),
        name="mod",
    )(conds, w_mod, b_mod.reshape(1, cols))


def _repack_kernel(tbl_ref, w_ref, o_ref):
    del tbl_ref
    o_ref[...] = w_ref[...].T.astype(BF16)


def _repack_weights(w_t, src_rows, tn):
    _, d = w_t.shape
    nblk = len(src_rows)
    grid_spec = pltpu.PrefetchScalarGridSpec(
        num_scalar_prefetch=1,
        grid=(nblk,),
        in_specs=[pl.BlockSpec((pl.Element(tn), pl.Element(d)),
                               lambda j, tbl: (pl.multiple_of(tbl[j], F32_SUBLANES), 0))],
        out_specs=pl.BlockSpec((d, tn), lambda j, tbl: (0, j)),
    )
    return pl.pallas_call(
        _repack_kernel,
        grid_spec=grid_spec,
        out_shape=jax.ShapeDtypeStruct((d, nblk * tn), BF16),
        name="repack",
    )(jnp.asarray(src_rows, jnp.int32), w_t)


def _repack_logits_kernel(w_ref, o_ref, *, n_cols):
    wt = w_ref[...].T
    lane = lax.broadcasted_iota(jnp.int32, wt.shape, 1)
    hi, lo = _split_bf16(jnp.where(lane < n_cols, wt, 0.0), 2)
    o_ref[:, 0:LANES] = hi
    o_ref[:, LANES:2 * LANES] = lo


def _repack_logit_weights(w_t, row0, n_cols):
    _, d = w_t.shape
    return pl.pallas_call(
        functools.partial(_repack_logits_kernel, n_cols=n_cols),
        grid=(1,),
        in_specs=[pl.BlockSpec((pl.Element(LANES), pl.Element(d)), lambda j: (row0, 0))],
        out_specs=pl.BlockSpec((d, 2 * LANES), lambda j: (0, 0)),
        out_shape=jax.ShapeDtypeStruct((d, 2 * LANES), BF16),
        name="repack_logits",
    )(w_t)


def _proj_kernel(x_ref, xp_ref, xn_ref, mod_ref, gpre_ref, w_ref, wab_ref, wconv_ref, y_ref, ab_ref, *,
                 kinds, col0, tiles_per_seq):
    tm = x_ref.shape[0]
    halo = xp_ref.shape[0]
    pos = lax.rem(pl.program_id(0), tiles_per_seq)
    xa = jnp.concatenate([xp_ref[...], x_ref[...], xn_ref[...]], axis=0)
    h = _prenorm(xa, gpre_ref[...], mod_ref[...])
    hb = h.astype(BF16)
    rows = lax.broadcasted_iota(jnp.int32, (tm, 1), 0)
    keep_dn = (rows != 0) | (pos != 0)
    keep_up = (rows != tm - 1) | (pos != tiles_per_seq - 1)
    for n, kind in enumerate(kinds):
        sl = slice(n * D_MODEL, (n + 1) * D_MODEL)
        wsl = slice(col0 + n * D_MODEL, col0 + (n + 1) * D_MODEL)
        p = jnp.dot(hb, w_ref[:, wsl], preferred_element_type=F32)
        wc = wconv_ref[:, wsl]
        p_dn = jnp.where(keep_dn, pltpu.roll(p, 1, axis=0)[halo:halo + tm], 0.0)
        p_up = jnp.where(keep_up, pltpu.roll(p, tm + 2 * halo - 1, axis=0)[halo:halo + tm], 0.0)
        y = _silu(p_dn * wc[0:1, :] + p[halo:halo + tm] * wc[1:2, :] + p_up * wc[2:3, :])
        if kind == "v":
            y_ref[:, sl] = y.astype(BF16)
            continue
        scale = HEAD_DIM ** -0.5 if kind == "q" else 1.0
        for hd in range(HEADS):
            yh = y[:, hd * HEAD_DIM:(hd + 1) * HEAD_DIM]
            inv = lax.rsqrt(jnp.sum(yh * yh, axis=-1, keepdims=True) + EPS) * scale
            y_ref[:, n * D_MODEL + hd * HEAD_DIM:n * D_MODEL + (hd + 1) * HEAD_DIM] = (yh * inv).astype(BF16)
    h_hi, h_lo = _split_bf16(h[halo:halo + tm], 2)
    hw = jnp.dot(h_hi, wab_ref[...], preferred_element_type=F32)
    ab_ref[...] = (hw[:, 0:LANES] + hw[:, LANES:2 * LANES]
                   + jnp.dot(h_lo, wab_ref[:, 0:LANES], preferred_element_type=F32))


def _project_qkv(x2d, mod3, tiles_per_mod, g_pre, w_pack, w_ab, w_conv, tm, seq, kinds):
    rows, d = x2d.shape
    nq = len(kinds) * d
    nw = 3 * d
    halo = F32_SUBLANES
    hb = tm // halo
    last_halo = rows // halo - 1
    kern = functools.partial(_proj_kernel, kinds=kinds, col0=nw - nq, tiles_per_seq=seq // tm)
    return pl.pallas_call(
        kern,
        grid=(rows // tm,),
        in_specs=[pl.BlockSpec((tm, d), lambda i: (i, 0)),
                  pl.BlockSpec((halo, d), lambda i: (jnp.maximum(i * hb - 1, 0), 0)),
                  pl.BlockSpec((halo, d), lambda i: (jnp.minimum((i + 1) * hb, last_halo), 0)),
                  pl.BlockSpec((None, 1, 3 * d), lambda i: (i // tiles_per_mod, 0, 0)),
                  pl.BlockSpec((1, d), lambda i: (0, 0)),
                  pl.BlockSpec((d, nw), lambda i: (0, 0)),
                  pl.BlockSpec((d, 2 * LANES), lambda i: (0, 0)),
                  pl.BlockSpec((3, nw), lambda i: (0, 0))],
        out_specs=[pl.BlockSpec((tm, nq), lambda i: (i, 0)),
                   pl.BlockSpec((tm, LANES), lambda i: (i, 0))],
        out_shape=[jax.ShapeDtypeStruct((rows, nq), BF16),
                   jax.ShapeDtypeStruct((rows, LANES), F32)],
        compiler_params=pltpu.CompilerParams(vmem_limit_bytes=VMEM_LIMIT),
        name="proj",
    )(x2d, x2d, x2d, mod3, g_pre, w_pack, w_ab, w_conv)


def _unit_tri_inverse(a_list, eye, ii, jj):
    c = a_list[0].shape[0]
    shift = BASE_BLOCK.bit_length() - 1
    same = (ii >> shift) == (jj >> shift)
    d1f = [jnp.where(same, a, 0.0) for a in a_list]
    d1 = [d.astype(BF16) for d in d1f]
    d2 = [jnp.dot(d, d, preferred_element_type=F32) for d in d1]
    d2b = [d.astype(BF16) for d in d2]
    d34 = [jnp.dot(jnp.concatenate([x, y], axis=0), y, preferred_element_type=F32)
           for x, y in zip(d1, d2b)]
    z = [eye - x + y - p[0:c] for x, y, p in zip(d1f, d2, d34)]
    t = [x + _bdot(x, p[c:2 * c]) for x, p in zip(z, d34)]
    b = BASE_BLOCK
    while b < c:
        s1 = b.bit_length() - 1
        off = ((ii >> (s1 + 1)) == (jj >> (s1 + 1))) & ((ii >> s1) != (jj >> s1))
        tb = [x.astype(BF16) for x in t]
        tc = [jnp.dot(x, jnp.where(off, a, 0.0).astype(BF16), preferred_element_type=F32)
              for x, a in zip(tb, a_list)]
        t = [x - jnp.dot(y.astype(BF16), xb, preferred_element_type=F32) for x, y, xb in zip(t, tc, tb)]
        b *= 2
    return t


def _gdn_kernel(*refs, reverse, with_q, accumulate, n_t, tg):
    it = iter(refs)
    y_ref, ab_ref, arow_ref, dtrow_ref, s0_ref = next(it), next(it), next(it), next(it), next(it)
    oin_ref = next(it) if accumulate else None
    o_ref = next(it) if with_q else None
    sfin_ref = None if with_q else next(it)
    s_scr = next(it)

    c = GDN_CHUNK
    i = pl.program_id(1)
    ncol = y_ref.shape[1]
    koff = ncol - 2 * D_MODEL
    voff = ncol - D_MODEL

    @pl.when(i == 0)
    def _():
        s_scr[...] = s0_ref[...]

    ab = ab_ref[...]
    g_cmp = -jnp.exp(arow_ref[...]) * _softplus(ab + dtrow_ref[...])
    beta_cmp = _sigmoid(ab)
    ti = lax.broadcasted_iota(jnp.int32, (tg, tg), 0)
    tj = lax.broadcasted_iota(jnp.int32, (tg, tg), 1)
    cs = c.bit_length() - 1
    same_chunk = (ti >> cs) == (tj >> cs)
    order = (tj >= ti) if reverse else (tj <= ti)
    cum_mat = jnp.where(same_chunk & order, 1.0, 0.0).astype(BF16)
    gcum3 = jnp.dot(cum_mat, jnp.concatenate(_split_bf16(g_cmp, 3), axis=1), preferred_element_type=F32)
    gcum_cmp = gcum3[:, 0:LANES] + gcum3[:, LANES:2 * LANES] + gcum3[:, 2 * LANES:3 * LANES]
    gcum_t = gcum_cmp.T
    eg_cmp = jnp.exp(gcum_cmp)
    dir_off = HEADS if reverse else 0

    ii = lax.broadcasted_iota(jnp.int32, (c, c), 0)
    jj = lax.broadcasted_iota(jnp.int32, (c, c), 1)
    eye = jnp.where(ii == jj, 1.0, 0.0).astype(F32)
    incl = (ii <= jj) if reverse else (ii >= jj)
    strict = (ii < jj) if reverse else (ii > jj)

    n_chunks = tg // c
    chunk_order = list(range(n_chunks - 1, -1, -1) if reverse else range(n_chunks))
    inst = [(cidx, h) for cidx in chunk_order for h in range(HEADS)]

    knb, qsb, be, gc, eg, decay = [], [], [], [], [], []
    for cidx, h in inst:
        rs = slice(cidx * c, (cidx + 1) * c)
        knb.append(y_ref[rs, koff + h * HEAD_DIM:koff + (h + 1) * HEAD_DIM])
        if with_q:
            qsb.append(y_ref[rs, h * HEAD_DIM:(h + 1) * HEAD_DIM])
        la = dir_off + h
        lb = 2 * HEADS + dir_off + h
        be.append(jnp.broadcast_to(beta_cmp[rs, lb:lb + 1], (c, HEAD_DIM)))
        g_col = jnp.broadcast_to(gcum_cmp[rs, la:la + 1], (c, HEAD_DIM))
        g_row = gcum_t[la:la + 1, cidx * c:(cidx + 1) * c]
        gc.append(g_col)
        decay.append(jnp.exp(jnp.where(incl, g_col - g_row, NEG_BIG)))
        eg.append(jnp.broadcast_to(eg_cmp[rs, la:la + 1], (c, HEAD_DIM)))
    kn = [k.astype(F32) for k in knb]
    kb = [k * b for k, b in zip(kn, be)]
    nt_dims = (((1,), (1,)), ((), ()))
    if with_q:
        qs = [q.astype(F32) for q in qsb]
        kk = [lax.dot_general(jnp.concatenate([a.astype(BF16), q], axis=0), k, nt_dims,
                              preferred_element_type=F32) for a, q, k in zip(kb, qsb, knb)]
        attn = [(x[c:2 * c] * d).astype(BF16) for x, d in zip(kk, decay)]
    else:
        kk = [lax.dot_general(a.astype(BF16), k, nt_dims, preferred_element_type=F32)
              for a, k in zip(kb, knb)]
    a_mat = [jnp.where(strict, x[0:c] * d, 0.0) for x, d in zip(kk, decay)]
    t_inv = _unit_tri_inverse(a_mat, eye, ii, jj)
    sol = []
    for n, (cidx, h) in enumerate(inst):
        rs = slice(cidx * c, (cidx + 1) * c)
        vv = y_ref[rs, voff + h * HEAD_DIM:voff + (h + 1) * HEAD_DIM].astype(F32)
        rhs = jnp.concatenate([vv * be[n], kb[n] * eg[n]], axis=1)
        sol.append(_bdot(t_inv[n], rhs))

    for ci in range(n_chunks):
        base = ci * HEADS
        idx = range(base, base + HEADS)
        rs = slice(chunk_order[ci] * c, (chunk_order[ci] + 1) * c)
        s_old = [s_scr[h] for h in range(HEADS)]
        s_b = [s.astype(BF16) for s in s_old]
        if with_q:
            ws = [jnp.dot(jnp.concatenate([sol[n][:, HEAD_DIM:], qs[n] * eg[n]], axis=0).astype(BF16), sb,
                          preferred_element_type=F32) for n, sb in zip(idx, s_b)]
        else:
            ws = [jnp.dot(sol[n][:, HEAD_DIM:].astype(BF16), sb, preferred_element_type=F32)
                  for n, sb in zip(idx, s_b)]
        v_new = [(sol[n][:, 0:HEAD_DIM] - w[0:c]).astype(BF16) for n, w in zip(idx, ws)]
        last = rs.start if reverse else rs.stop - 1
        gl_cmp = gcum_cmp[last:last + 1, :]
        tail_cmp = jnp.exp(gl_cmp - gcum_cmp[rs, :])
        egl_cmp = jnp.exp(gl_cmp)
        k_tail = [(kn[n] * jnp.broadcast_to(tail_cmp[:, dir_off + h:dir_off + h + 1], (c, HEAD_DIM))).T
                  .astype(BF16) for h, n in enumerate(idx)]
        if with_q:
            kva = [jnp.dot(jnp.concatenate([kt, attn[n]], axis=0), v, preferred_element_type=F32)
                   for kt, n, v in zip(k_tail, idx, v_new)]
        else:
            kva = [jnp.dot(kt, v, preferred_element_type=F32) for kt, v in zip(k_tail, v_new)]
        for h in range(HEADS):
            s_decay = jnp.broadcast_to(egl_cmp[:, dir_off + h:dir_off + h + 1], (HEAD_DIM, HEAD_DIM))
            s_scr[h] = s_old[h] * s_decay + kva[h][0:HEAD_DIM]
        if with_q:
            for h in range(HEADS):
                hs = slice(h * HEAD_DIM, (h + 1) * HEAD_DIM)
                o = ws[h][c:2 * c] + kva[h][HEAD_DIM:HEAD_DIM + c]
                if accumulate:
                    o = o + oin_ref[rs, hs].astype(F32)
                o_ref[rs, hs] = o.astype(o_ref.dtype)

    if not with_q:
        @pl.when(i == n_t - 1)
        def _():
            sfin_ref[...] = s_scr[...]


def _gdn_scan(y, ab, a_row, dt_row, s0, o_in, *, batch, reverse, with_q, tg):
    rows, ncol = y.shape
    n_t = rows // batch // tg
    accumulate = o_in is not None

    def tile(b, i):
        return (b * n_t + ((n_t - 1 - i) if reverse else i), 0)

    state = lambda b, i: (b, 0, 0, 0)
    const = lambda b, i: (0, 0)
    in_specs = [
        pl.BlockSpec((tg, ncol), tile),
        pl.BlockSpec((tg, LANES), tile),
        pl.BlockSpec((1, LANES), const),
        pl.BlockSpec((1, LANES), const),
        pl.BlockSpec((None, HEADS, HEAD_DIM, HEAD_DIM), state),
    ]
    args = [y, ab, a_row, dt_row, s0]
    if accumulate:
        in_specs.append(pl.BlockSpec((tg, D_MODEL), tile))
        args.append(o_in)
    if with_q:
        out_specs = pl.BlockSpec((tg, D_MODEL), tile)
        out_shape = jax.ShapeDtypeStruct((rows, D_MODEL), BF16)
    else:
        out_specs = pl.BlockSpec((None, HEADS, HEAD_DIM, HEAD_DIM), state)
        out_shape = jax.ShapeDtypeStruct((batch, HEADS, HEAD_DIM, HEAD_DIM), F32)
    kern = functools.partial(_gdn_kernel, reverse=reverse, with_q=with_q, accumulate=accumulate,
                             n_t=n_t, tg=tg)
    return pl.pallas_call(
        kern,
        grid=(batch, n_t),
        in_specs=in_specs,
        out_specs=out_specs,
        out_shape=out_shape,
        scratch_shapes=[pltpu.VMEM((HEADS, HEAD_DIM, HEAD_DIM), F32)],
        compiler_params=pltpu.CompilerParams(dimension_semantics=("arbitrary", "arbitrary"),
                                             vmem_limit_bytes=VMEM_LIMIT),
        name="gdn_bwd" if reverse else "gdn_fwd",
    )(*args)


def _brancha_kernel(x_ref, mod_ref, gpre_ref, wa_ref, lng_ref, lnb_ref, wsp_ref, bsp_ref, wpa_ref, ma_ref):
    d = D_MODEL
    tm = x_ref.shape[0]
    hb = _prenorm(x_ref[...], gpre_ref[...], mod_ref[...]).astype(BF16)
    vg = _gelu_tanh(jnp.dot(hb, wa_ref[:, d:2 * d], preferred_element_type=F32))
    mu = jnp.mean(vg, axis=-1, keepdims=True)
    vc = vg - mu
    var = jnp.mean(vc * vc, axis=-1, keepdims=True)
    v = (vc * lax.rsqrt(var + EPS) * lng_ref[...] + lnb_ref[...]).astype(BF16)
    parts = []
    for n in range(tm // GM_CHUNK):
        rs = slice(n * GM_CHUNK, (n + 1) * GM_CHUNK)
        cols = [jnp.dot(wsp_ref[g], v[rs, g * HEAD_DIM:(g + 1) * HEAD_DIM], preferred_element_type=F32)
                for g in range(d // HEAD_DIM)]
        parts.append(jnp.concatenate(cols, axis=1) + bsp_ref[...])
    s = jnp.concatenate(parts, axis=0)
    u = _gelu_tanh(jnp.dot(hb, wa_ref[:, 0:d], preferred_element_type=F32))
    z = _silu(jnp.dot(hb, wa_ref[:, 2 * d:3 * d], preferred_element_type=F32))
    ya = (u * s * z).astype(BF16)
    gate = _sigmoid(jnp.dot(hb, wa_ref[:, 3 * d:4 * d], preferred_element_type=F32))
    ma_ref[...] = (gate * jnp.dot(ya, wpa_ref[...], preferred_element_type=F32)).astype(ma_ref.dtype)


def _branch_a(x2d, mod3, tiles_per_mod, g_pre, w_pack, ln_g, ln_b, w_sp, b_sp_e, w_pa, tm):
    rows, d = x2d.shape
    const = lambda i: (0, 0)
    return pl.pallas_call(
        _brancha_kernel,
        grid=(rows // tm,),
        in_specs=[pl.BlockSpec((tm, d), lambda i: (i, 0)),
                  pl.BlockSpec((None, 1, 3 * d), lambda i: (i // tiles_per_mod, 0, 0)),
                  pl.BlockSpec((1, d), const),
                  pl.BlockSpec((d, 4 * d), lambda i: (0, 1)),
                  pl.BlockSpec((1, d), const),
                  pl.BlockSpec((1, d), const),
                  pl.BlockSpec(w_sp.shape, lambda i: (0, 0, 0)),
                  pl.BlockSpec((GM_CHUNK, d), const),
                  pl.BlockSpec((d, d), const)],
        out_specs=pl.BlockSpec((tm, d), lambda i: (i, 0)),
        out_shape=jax.ShapeDtypeStruct((rows, d), BF16),
        compiler_params=pltpu.CompilerParams(vmem_limit_bytes=VMEM_LIMIT),
        name="brancha",
    )(x2d, mod3, g_pre, w_pack, ln_g, ln_b, w_sp, b_sp_e, w_pa)


def _out_kernel(x_ref, mod_ref, gpre_ref, gpost_ref, o_ref, ma_ref, wzb_ref, wgb_ref, wpb_ref, wout_ref,
                gon_ref, out_ref):
    d = D_MODEL
    x = x_ref[...]
    mod = mod_ref[...]
    hb = _prenorm(x, gpre_ref[...], mod).astype(BF16)
    zb = _silu(jnp.dot(hb, wzb_ref[...], preferred_element_type=F32))
    gon = gon_ref[...]
    parts = []
    for h in range(HEADS):
        oh = o_ref[:, h * HEAD_DIM:(h + 1) * HEAD_DIM].astype(F32)
        ms = jnp.mean(oh * oh, axis=-1, keepdims=True)
        parts.append(oh * lax.rsqrt(ms + EPS) * gon)
    yb = (jnp.concatenate(parts, axis=1) * zb).astype(BF16)
    gate_b = _sigmoid(jnp.dot(hb, wgb_ref[...], preferred_element_type=F32))
    merged = ma_ref[...].astype(F32) + gate_b * jnp.dot(yb, wpb_ref[...], preferred_element_type=F32)
    z = jnp.dot(merged.astype(BF16), wout_ref[...], preferred_element_type=F32)
    zn = z * lax.rsqrt(jnp.mean(z * z, axis=-1, keepdims=True) + EPS) * gpost_ref[...]
    out_ref[...] = x + mod[:, 2 * d:3 * d] * zn


def _merge_out(x2d, mod3, tiles_per_mod, g_pre, g_post, o, m_a, w_pack, w_pb, w_out, g_onorm, tm):
    rows, d = x2d.shape
    const = lambda i: (0, 0)
    row = lambda i: (i, 0)
    return pl.pallas_call(
        _out_kernel,
        grid=(rows // tm,),
        in_specs=[pl.BlockSpec((tm, d), row),
                  pl.BlockSpec((None, 1, 3 * d), lambda i: (i // tiles_per_mod, 0, 0)),
                  pl.BlockSpec((1, d), const),
                  pl.BlockSpec((1, d), const),
                  pl.BlockSpec((tm, d), row),
                  pl.BlockSpec((tm, d), row),
                  pl.BlockSpec((d, d), lambda i: (0, 3)),
                  pl.BlockSpec((d, d), lambda i: (0, 8)),
                  pl.BlockSpec((d, d), const),
                  pl.BlockSpec((d, d), const),
                  pl.BlockSpec((1, HEAD_DIM), const)],
        out_specs=pl.BlockSpec((tm, d), row),
        out_shape=jax.ShapeDtypeStruct((rows, d), F32),
        compiler_params=pltpu.CompilerParams(vmem_limit_bytes=VMEM_LIMIT),
        name="merge_out",
    )(x2d, mod3, g_pre, g_post, o, m_a, w_pack, w_pack, w_pb, w_out, g_onorm)


def _layer(x, ctx, mod_x, mod_c, g_pre, g_post, w_in, w_conv, a_log, dt_bias, g_onorm, gm_ln_g, gm_ln_b,
           w_sp, b_sp, w_pa, w_pb, w_out):
    batch, seq, d = x.shape
    ctx_len = ctx.shape[1]
    kd = HEADS * HEAD_DIM
    off_a = 3 * kd
    off_zb = off_a + 4 * HEADS

    tn = 1024
    w_t = jnp.transpose(w_in)
    w_pack = _repack_weights(w_t, list(range(0, off_a, tn)) + list(range(off_zb, w_in.shape[1], tn)), tn)
    w_ab2 = _repack_logit_weights(w_t, off_a, off_zb - off_a)
    a_row = jnp.pad(a_log.reshape(1, 2 * HEADS), ((0, 0), (0, LANES - 2 * HEADS)))
    dt_row = jnp.pad(dt_bias.reshape(1, 2 * HEADS), ((0, 0), (0, LANES - 2 * HEADS)))
    b_sp_e = jnp.repeat(b_sp.T, HEAD_DIM, axis=1)
    g_pre2 = g_pre.reshape(1, d)

    x2d = x.reshape(batch * seq, d)
    c2d = ctx.reshape(batch * ctx_len, d)
    tm = 1024

    kv_c, ab_c = _project_qkv(c2d, mod_c, 1 << 30, g_pre2, w_pack, w_ab2, w_conv, ctx_len, ctx_len, ("k", "v"))
    s_zero = jnp.zeros((batch, HEADS, HEAD_DIM, HEAD_DIM), F32)
    s_fwd = _gdn_scan(kv_c, ab_c, a_row, dt_row, s_zero, None,
                      batch=batch, reverse=False, with_q=False, tg=ctx_len)
    s_bwd = _gdn_scan(kv_c, ab_c, a_row, dt_row, s_zero, None,
                      batch=batch, reverse=True, with_q=False, tg=ctx_len)

    qkv, ab = _project_qkv(x2d, mod_x, seq // tm, g_pre2, w_pack, w_ab2, w_conv, tm, seq, ("q", "k", "v"))
    tg = 2 * GDN_CHUNK
    o_f = _gdn_scan(qkv, ab, a_row, dt_row, s_fwd, None, batch=batch, reverse=False, with_q=True, tg=tg)
    o = _gdn_scan(qkv, ab, a_row, dt_row, s_bwd, o_f, batch=batch, reverse=True, with_q=True, tg=tg)
    m_a = _branch_a(x2d, mod_x, seq // tm, g_pre2, w_pack, gm_ln_g.reshape(1, d), gm_ln_b.reshape(1, d),
                    w_sp.astype(BF16), b_sp_e, w_pa.astype(BF16), tm)
    out = _merge_out(x2d, mod_x, seq // tm, g_pre2, g_post.reshape(1, d), o, m_a, w_pack,
                     w_pb.astype(BF16), w_out.astype(BF16), g_onorm.reshape(1, HEAD_DIM), tm)
    return out.reshape(batch, seq, d)


def kernel(x, c, ctx, c_ctx, w_mod, b_mod, g_pre, g_post, w_in, w_conv, a_log, dt_bias, g_onorm, gm_ln_g,
           gm_ln_b, w_sp, b_sp, w_pa, w_pb, w_out):
    batch, _, d = x.shape
    depth = w_mod.shape[0]
    assert depth == 1, "context tokens are only read (never updated) by a single-layer stack"
    assert ctx.shape[1] % GDN_CHUNK == 0 and x.shape[1] % (8 * GDN_CHUNK) == 0 and d == D_MODEL
    pad = (-(batch + 1)) % 8
    conds = jnp.concatenate([c, c_ctx[None, :], jnp.zeros((pad, d), F32)], axis=0)
    i = 0
    m = _modulation(conds, w_mod[i], b_mod[i])
    mod_x = m[0:batch].reshape(batch, 1, 3 * d)
    mod_c = m[batch:batch + 1].reshape(1, 1, 3 * d)
    return _layer(x, ctx, mod_x, mod_c, g_pre[i], g_post[i], w_in[i], w_conv[i], a_log[i], dt_bias[i],
                  g_onorm[i], gm_ln_g[i], gm_ln_b[i], w_sp[i], b_sp[i], w_pa[i], w_pb[i], w_out[i])
```

```python
import functools

import jax
import jax.numpy as jnp
from jax import lax
from jax.experimental import pallas as pl
from jax.experimental.pallas import tpu as pltpu

F32 = jnp.float32
BF16 = jnp.bfloat16
HIGHEST = lax.Precision.HIGHEST

D_MODEL = 1024
HEADS = 8
HEAD_DIM = 128
GM_CHUNK = 128
EPS = 1e-6
NEG_BIG = -1e30

LANES = 128
F32_SUBLANES = 8
GDN_CHUNK = 128
BASE_BLOCK = 8
VMEM_LIMIT = 56 * 1024 * 1024


def _sigmoid(x):
    return 0.5 + 0.5 * jnp.tanh(0.5 * x)


def _silu(x):
    h = 0.5 * x
    return h + h * jnp.tanh(h)


def _gelu_tanh(x):
    c = 0.7978845608028654
    return 0.5 * x * (1.0 + jnp.tanh(c * (x + 0.044715 * (x * x * x))))


def _softplus(x):
    return jnp.maximum(x, 0.0) + jnp.log1p(jnp.exp(-jnp.abs(x)))


def _prenorm(x, g_pre, mod):
    ms = jnp.mean(x * x, axis=-1, keepdims=True)
    y = x * lax.rsqrt(ms + EPS) * g_pre
    return y * (1.0 + mod[:, D_MODEL:2 * D_MODEL]) + mod[:, 0:D_MODEL]


def _bdot(a, b):
    return jnp.dot(a.astype(BF16), b.astype(BF16), preferred_element_type=F32)


def _split_bf16(x, parts):
    out, r = [], x
    for _ in range(parts):
        p = r.astype(BF16)
        out.append(p)
        r = r - p.astype(F32)
    return out


def _mod_kernel(c_ref, w_ref, b_ref, o_ref):
    s = _silu(c_ref[...])
    o_ref[...] = jnp.dot(s, w_ref[...], preferred_element_type=F32, precision=HIGHEST) + b_ref[...]


def _modulation(conds, w_mod, b_mod):
    n, d = conds.shape
    cols = w_mod.shape[1]
    tn = 1024
    return pl.pallas_call(
        _mod_kernel,
        grid=(cols // tn,),
        in_specs=[pl.BlockSpec((n, d), lambda j: (0, 0)),
                  pl.BlockSpec((d, tn), lambda j: (0, j)),
                  pl.BlockSpec((1, tn), lambda j: (0, j))],
        out_specs=pl.BlockSpec((n, tn), lambda j: (0, j)),
        out_shape=jax.ShapeDtypeStruct((n, cols), F32),
        name="mod",
    )(conds, w_mod, b_mod.reshape(1, cols))


def _repack_kernel(tbl_ref, w_ref, o_ref):
    del tbl_ref
    o_ref[...] = w_ref[...].T.astype(BF16)


def _repack_weights(w_t, src_rows, tn):
    _, d = w_t.shape
    nblk = len(src_rows)
    grid_spec = pltpu.PrefetchScalarGridSpec(
        num_scalar_prefetch=1,
        grid=(nblk,),
        in_specs=[pl.BlockSpec((pl.Element(tn), pl.Element(d)),
                               lambda j, tbl: (pl.multiple_of(tbl[j], F32_SUBLANES), 0))],
        out_specs=pl.BlockSpec((d, tn), lambda j, tbl: (0, j)),
    )
    return pl.pallas_call(
        _repack_kernel,
        grid_spec=grid_spec,
        out_shape=jax.ShapeDtypeStruct((d, nblk * tn), BF16),
        name="repack",
    )(jnp.asarray(src_rows, jnp.int32), w_t)


def _repack_logits_kernel(w_ref, o_ref, *, n_cols):
    wt = w_ref[...].T
    lane = lax.broadcasted_iota(jnp.int32, wt.shape, 1)
    hi, lo = _split_bf16(jnp.where(lane < n_cols, wt, 0.0), 2)
    o_ref[:, 0:LANES] = hi
    o_ref[:, LANES:2 * LANES] = lo


def _repack_logit_weights(w_t, row0, n_cols):
    _, d = w_t.shape
    return pl.pallas_call(
        functools.partial(_repack_logits_kernel, n_cols=n_cols),
        grid=(1,),
        in_specs=[pl.BlockSpec((pl.Element(LANES), pl.Element(d)), lambda j: (row0, 0))],
        out_specs=pl.BlockSpec((d, 2 * LANES), lambda j: (0, 0)),
        out_shape=jax.ShapeDtypeStruct((d, 2 * LANES), BF16),
        name="repack_logits",
    )(w_t)


def _proj_kernel(x_ref, xp_ref, xn_ref, mod_ref, gpre_ref, w_ref, wab_ref, wconv_ref, y_ref, ab_ref, *,
                 kinds, col0, tiles_per_seq):
    tm = x_ref.shape[0]
    halo = xp_ref.shape[0]
    pos = lax.rem(pl.program_id(0), tiles_per_seq)
    xa = jnp.concatenate([xp_ref[...], x_ref[...], xn_ref[...]], axis=0)
    h = _prenorm(xa, gpre_ref[...], mod_ref[...])
    hb = h.astype(BF16)
    rows = lax.broadcasted_iota(jnp.int32, (tm, 1), 0)
    keep_dn = (rows != 0) | (pos != 0)
    keep_up = (rows != tm - 1) | (pos != tiles_per_seq - 1)
    for n, kind in enumerate(kinds):
        sl = slice(n * D_MODEL, (n + 1) * D_MODEL)
        wsl = slice(col0 + n * D_MODEL, col0 + (n + 1) * D_MODEL)
        p = jnp.dot(hb, w_ref[:, wsl], preferred_element_type=F32)
        wc = wconv_ref[:, wsl]
        p_dn = jnp.where(keep_dn, pltpu.roll(p, 1, axis=0)[halo:halo + tm], 0.0)
        p_up = jnp.where(keep_up, pltpu.roll(p, tm + 2 * halo - 1, axis=0)[halo:halo + tm], 0.0)
        y = _silu(p_dn * wc[0:1, :] + p[halo:halo + tm] * wc[1:2, :] + p_up * wc[2:3, :])
        if kind == "v":
            y_ref[:, sl] = y.astype(BF16)
            continue
        scale = HEAD_DIM ** -0.5 if kind == "q" else 1.0
        for hd in range(HEADS):
            yh = y[:, hd * HEAD_DIM:(hd + 1) * HEAD_DIM]
            inv = lax.rsqrt(jnp.sum(yh * yh, axis=-1, keepdims=True) + EPS) * scale
            y_ref[:, n * D_MODEL + hd * HEAD_DIM:n * D_MODEL + (hd + 1) * HEAD_DIM] = (yh * inv).astype(BF16)
    h_hi, h_lo = _split_bf16(h[halo:halo + tm], 2)
    hw = jnp.dot(h_hi, wab_ref[...], preferred_element_type=F32)
    ab_ref[...] = (hw[:, 0:LANES] + hw[:, LANES:2 * LANES]
                   + jnp.dot(h_lo, wab_ref[:, 0:LANES], preferred_element_type=F32))


def _project_qkv(x2d, mod3, tiles_per_mod, g_pre, w_pack, w_ab, w_conv, tm, seq, kinds):
    rows, d = x2d.shape
    nq = len(kinds) * d
    nw = 3 * d
    halo = F32_SUBLANES
    hb = tm // halo
    last_halo = rows // halo - 1
    kern = functools.partial(_proj_kernel, kinds=kinds, col0=nw - nq, tiles_per_seq=seq // tm)
    return pl.pallas_call(
        kern,
        grid=(rows // tm,),
        in_specs=[pl.BlockSpec((tm, d), lambda i: (i, 0)),
                  pl.BlockSpec((halo, d), lambda i: (jnp.maximum(i * hb - 1, 0), 0)),
                  pl.BlockSpec((halo, d), lambda i: (jnp.minimum((i + 1) * hb, last_halo), 0)),
                  pl.BlockSpec((None, 1, 3 * d), lambda i: (i // tiles_per_mod, 0, 0)),
                  pl.BlockSpec((1, d), lambda i: (0, 0)),
                  pl.BlockSpec((d, nw), lambda i: (0, 0)),
                  pl.BlockSpec((d, 2 * LANES), lambda i: (0, 0)),
                  pl.BlockSpec((3, nw), lambda i: (0, 0))],
        out_specs=[pl.BlockSpec((tm, nq), lambda i: (i, 0)),
                   pl.BlockSpec((tm, LANES), lambda i: (i, 0))],
        out_shape=[jax.ShapeDtypeStruct((rows, nq), BF16),
                   jax.ShapeDtypeStruct((rows, LANES), F32)],
        compiler_params=pltpu.CompilerParams(vmem_limit_bytes=VMEM_LIMIT),
        name="proj",
    )(x2d, x2d, x2d, mod3, g_pre, w_pack, w_ab, w_conv)


def _unit_tri_inverse(a_list, eye, ii, jj):
    c = a_list[0].shape[0]
    shift = BASE_BLOCK.bit_length() - 1
    same = (ii >> shift) == (jj >> shift)
    d1f = [jnp.where(same, a, 0.0) for a in a_list]
    d1 = [d.astype(BF16) for d in d1f]
    d2 = [jnp.dot(d, d, preferred_element_type=F32) for d in d1]
    d2b = [d.astype(BF16) for d in d2]
    d34 = [jnp.dot(jnp.concatenate([x, y], axis=0), y, preferred_element_type=F32)
           for x, y in zip(d1, d2b)]
    z = [eye - x + y - p[0:c] for x, y, p in zip(d1f, d2, d34)]
    t = [x + _bdot(x, p[c:2 * c]) for x, p in zip(z, d34)]
    b = BASE_BLOCK
    while b < c:
        s1 = b.bit_length() - 1
        off = ((ii >> (s1 + 1)) == (jj >> (s1 + 1))) & ((ii >> s1) != (jj >> s1))
        tb = [x.astype(BF16) for x in t]
        tc = [jnp.dot(x, jnp.where(off, a, 0.0).astype(BF16), preferred_element_type=F32)
              for x, a in zip(tb, a_list)]
        t = [x - jnp.dot(y.astype(BF16), xb, preferred_element_type=F32) for x, y, xb in zip(t, tc, tb)]
        b *= 2
    return t


def _gdn_kernel(*refs, reverse, with_q, accumulate, n_t, tg):
    it = iter(refs)
    y_ref, ab_ref, arow_ref, dtrow_ref, s0_ref = next(it), next(it), next(it), next(it), next(it)
    oin_ref = next(it) if accumulate else None
    o_ref = next(it) if with_q else None
    sfin_ref = None if with_q else next(it)
    s_scr = next(it)

    c = GDN_CHUNK
    i = pl.program_id(1)
    ncol = y_ref.shape[1]
    koff = ncol - 2 * D_MODEL
    voff = ncol - D_MODEL

    @pl.when(i == 0)
    def _():
        s_scr[...] = s0_ref[...]

    ab = ab_ref[...]
    g_cmp = -jnp.exp(arow_ref[...]) * _softplus(ab + dtrow_ref[...])
    beta_cmp = _sigmoid(ab)
    ti = lax.broadcasted_iota(jnp.int32, (tg, tg), 0)
    tj = lax.broadcasted_iota(jnp.int32, (tg, tg), 1)
    cs = c.bit_length() - 1
    same_chunk = (ti >> cs) == (tj >> cs)
    order = (tj >= ti) if reverse else (tj <= ti)
    cum_mat = jnp.where(same_chunk & order, 1.0, 0.0).astype(BF16)
    gcum3 = jnp.dot(cum_mat, jnp.concatenate(_split_bf16(g_cmp, 3), axis=1), preferred_element_type=F32)
    gcum_cmp = gcum3[:, 0:LANES] + gcum3[:, LANES:2 * LANES] + gcum3[:, 2 * LANES:3 * LANES]
    gcum_t = gcum_cmp.T
    eg_cmp = jnp.exp(gcum_cmp)
    dir_off = HEADS if reverse else 0

    ii = lax.broadcasted_iota(jnp.int32, (c, c), 0)
    jj = lax.broadcasted_iota(jnp.int32, (c, c), 1)
    eye = jnp.where(ii == jj, 1.0, 0.0).astype(F32)
    incl = (ii <= jj) if reverse else (ii >= jj)
    strict = (ii < jj) if reverse else (ii > jj)

    n_chunks = tg // c
    chunk_order = list(range(n_chunks - 1, -1, -1) if reverse else range(n_chunks))
    inst = [(cidx, h) for cidx in chunk_order for h in range(HEADS)]

    knb, qsb, be, gc, eg, decay = [], [], [], [], [], []
    for cidx, h in inst:
        rs = slice(cidx * c, (cidx + 1) * c)
        knb.append(y_ref[rs, koff + h * HEAD_DIM:koff + (h + 1) * HEAD_DIM])
        if with_q:
            qsb.append(y_ref[rs, h * HEAD_DIM:(h + 1) * HEAD_DIM])
        la = dir_off + h
        lb = 2 * HEADS + dir_off + h
        be.append(jnp.broadcast_to(beta_cmp[rs, lb:lb + 1], (c, HEAD_DIM)))
        g_col = jnp.broadcast_to(gcum_cmp[rs, la:la + 1], (c, HEAD_DIM))
        g_row = gcum_t[la:la + 1, cidx * c:(cidx + 1) * c]
        gc.append(g_col)
        decay.append(jnp.exp(jnp.where(incl, g_col - g_row, NEG_BIG)))
        eg.append(jnp.broadcast_to(eg_cmp[rs, la:la + 1], (c, HEAD_DIM)))
    kn = [k.astype(F32) for k in knb]
    kb = [k * b for k, b in zip(kn, be)]
    nt_dims = (((1,), (1,)), ((), ()))
    if with_q:
        qs = [q.astype(F32) for q in qsb]
        kk = [lax.dot_general(jnp.concatenate([a.astype(BF16), q], axis=0), k, nt_dims,
                              preferred_element_type=F32) for a, q, k in zip(kb, qsb, knb)]
        attn = [(x[c:2 * c] * d).astype(BF16) for x, d in zip(kk, decay)]
    else:
        kk = [lax.dot_general(a.astype(BF16), k, nt_dims, preferred_element_type=F32)
              for a, k in zip(kb, knb)]
    a_mat = [jnp.where(strict, x[0:c] * d, 0.0) for x, d in zip(kk, decay)]
    t_inv = _unit_tri_inverse(a_mat, eye, ii, jj)
    sol = []
    for n, (cidx, h) in enumerate(inst):
        rs = slice(cidx * c, (cidx + 1) * c)
        vv = y_ref[rs, voff + h * HEAD_DIM:voff + (h + 1) * HEAD_DIM].astype(F32)
        rhs = jnp.concatenate([vv * be[n], kb[n] * eg[n]], axis=1)
        sol.append(_bdot(t_inv[n], rhs))

    for ci in range(n_chunks):
        base = ci * HEADS
        idx = range(base, base + HEADS)
        rs = slice(chunk_order[ci] * c, (chunk_order[ci] + 1) * c)
        s_old = [s_scr[h] for h in range(HEADS)]
        s_b = [s.astype(BF16) for s in s_old]
        if with_q:
            ws = [jnp.dot(jnp.concatenate([sol[n][:, HEAD_DIM:], qs[n] * eg[n]], axis=0).astype(BF16), sb,
                          preferred_element_type=F32) for n, sb in zip(idx, s_b)]
        else:
            ws = [jnp.dot(sol[n][:, HEAD_DIM:].astype(BF16), sb, preferred_element_type=F32)
                  for n, sb in zip(idx, s_b)]
        v_new = [(sol[n][:, 0:HEAD_DIM] - w[0:c]).astype(BF16) for n, w in zip(idx, ws)]
        last = rs.start if reverse else rs.stop - 1
        gl_cmp = gcum_cmp[last:last + 1, :]
        tail_cmp = jnp.exp(gl_cmp - gcum_cmp[rs, :])
        egl_cmp = jnp.exp(gl_cmp)
        k_tail = [(kn[n] * jnp.broadcast_to(tail_cmp[:, dir_off + h:dir_off + h + 1], (c, HEAD_DIM))).T
                  .astype(BF16) for h, n in enumerate(idx)]
        if with_q:
            kva = [jnp.dot(jnp.concatenate([kt, attn[n]], axis=0), v, preferred_element_type=F32)
                   for kt, n, v in zip(k_tail, idx, v_new)]
        else:
            kva = [jnp.dot(kt, v, preferred_element_type=F32) for kt, v in zip(k_tail, v_new)]
        for h in range(HEADS):
            s_decay = jnp.broadcast_to(egl_cmp[:, dir_off + h:dir_off + h + 1], (HEAD_DIM, HEAD_DIM))
            s_scr[h] = s_old[h] * s_decay + kva[h][0:HEAD_DIM]
        if with_q:
            for h in range(HEADS):
                hs = slice(h * HEAD_DIM, (h + 1) * HEAD_DIM)
                o = ws[h][c:2 * c] + kva[h][HEAD_DIM:HEAD_DIM + c]
                if accumulate:
                    o = o + oin_ref[rs, hs]
                o_ref[rs, hs] = o

    if not with_q:
        @pl.when(i == n_t - 1)
        def _():
            sfin_ref[...] = s_scr[...]


def _gdn_scan(y, ab, a_row, dt_row, s0, o_in, *, batch, reverse, with_q, tg):
    rows, ncol = y.shape
    n_t = rows // batch // tg
    accumulate = o_in is not None

    def tile(b, i):
        return (b * n_t + ((n_t - 1 - i) if reverse else i), 0)

    state = lambda b, i: (b, 0, 0, 0)
    const = lambda b, i: (0, 0)
    in_specs = [
        pl.BlockSpec((tg, ncol), tile),
        pl.BlockSpec((tg, LANES), tile),
        pl.BlockSpec((1, LANES), const),
        pl.BlockSpec((1, LANES), const),
        pl.BlockSpec((None, HEADS, HEAD_DIM, HEAD_DIM), state),
    ]
    args = [y, ab, a_row, dt_row, s0]
    if accumulate:
        in_specs.append(pl.BlockSpec((tg, D_MODEL), tile))
        args.append(o_in)
    if with_q:
        out_specs = pl.BlockSpec((tg, D_MODEL), tile)
        out_shape = jax.ShapeDtypeStruct((rows, D_MODEL), F32)
    else:
        out_specs = pl.BlockSpec((None, HEADS, HEAD_DIM, HEAD_DIM), state)
        out_shape = jax.ShapeDtypeStruct((batch, HEADS, HEAD_DIM, HEAD_DIM), F32)
    kern = functools.partial(_gdn_kernel, reverse=reverse, with_q=with_q, accumulate=accumulate,
                             n_t=n_t, tg=tg)
    return pl.pallas_call(
        kern,
        grid=(batch, n_t),
        in_specs=in_specs,
        out_specs=out_specs,
        out_shape=out_shape,
        scratch_shapes=[pltpu.VMEM((HEADS, HEAD_DIM, HEAD_DIM), F32)],
        compiler_params=pltpu.CompilerParams(dimension_semantics=("arbitrary", "arbitrary"),
                                             vmem_limit_bytes=VMEM_LIMIT),
        name="gdn_bwd" if reverse else "gdn_fwd",
    )(*args)


def _brancha_kernel(x_ref, mod_ref, gpre_ref, wa_ref, lng_ref, lnb_ref, wsp_ref, bsp_ref, wpa_ref, ma_ref):
    d = D_MODEL
    tm = x_ref.shape[0]
    hb = _prenorm(x_ref[...], gpre_ref[...], mod_ref[...]).astype(BF16)
    vg = _gelu_tanh(jnp.dot(hb, wa_ref[:, d:2 * d], preferred_element_type=F32))
    mu = jnp.mean(vg, axis=-1, keepdims=True)
    vc = vg - mu
    var = jnp.mean(vc * vc, axis=-1, keepdims=True)
    v = (vc * lax.rsqrt(var + EPS) * lng_ref[...] + lnb_ref[...]).astype(BF16)
    parts = []
    for n in range(tm // GM_CHUNK):
        rs = slice(n * GM_CHUNK, (n + 1) * GM_CHUNK)
        cols = [jnp.dot(wsp_ref[g], v[rs, g * HEAD_DIM:(g + 1) * HEAD_DIM], preferred_element_type=F32)
                for g in range(d // HEAD_DIM)]
        parts.append(jnp.concatenate(cols, axis=1) + bsp_ref[...])
    s = jnp.concatenate(parts, axis=0)
    u = _gelu_tanh(jnp.dot(hb, wa_ref[:, 0:d], preferred_element_type=F32))
    z = _silu(jnp.dot(hb, wa_ref[:, 2 * d:3 * d], preferred_element_type=F32))
    ya = (u * s * z).astype(BF16)
    gate = _sigmoid(jnp.dot(hb, wa_ref[:, 3 * d:4 * d], preferred_element_type=F32))
    ma_ref[...] = gate * jnp.dot(ya, wpa_ref[...], preferred_element_type=F32)


def _branch_a(x2d, mod3, tiles_per_mod, g_pre, w_pack, ln_g, ln_b, w_sp, b_sp_e, w_pa, tm):
    rows, d = x2d.shape
    const = lambda i: (0, 0)
    return pl.pallas_call(
        _brancha_kernel,
        grid=(rows // tm,),
        in_specs=[pl.BlockSpec((tm, d), lambda i: (i, 0)),
                  pl.BlockSpec((None, 1, 3 * d), lambda i: (i // tiles_per_mod, 0, 0)),
                  pl.BlockSpec((1, d), const),
                  pl.BlockSpec((d, 4 * d), lambda i: (0, 1)),
                  pl.BlockSpec((1, d), const),
                  pl.BlockSpec((1, d), const),
                  pl.BlockSpec(w_sp.shape, lambda i: (0, 0, 0)),
                  pl.BlockSpec((GM_CHUNK, d), const),
                  pl.BlockSpec((d, d), const)],
        out_specs=pl.BlockSpec((tm, d), lambda i: (i, 0)),
        out_shape=jax.ShapeDtypeStruct((rows, d), F32),
        compiler_params=pltpu.CompilerParams(vmem_limit_bytes=VMEM_LIMIT),
        name="brancha",
    )(x2d, mod3, g_pre, w_pack, ln_g, ln_b, w_sp, b_sp_e, w_pa)


def _out_kernel(x_ref, mod_ref, gpre_ref, gpost_ref, o_ref, ma_ref, wzb_ref, wgb_ref, wpb_ref, wout_ref,
                gon_ref, out_ref):
    d = D_MODEL
    x = x_ref[...]
    mod = mod_ref[...]
    hb = _prenorm(x, gpre_ref[...], mod).astype(BF16)
    zb = _silu(jnp.dot(hb, wzb_ref[...], preferred_element_type=F32))
    gon = gon_ref[...]
    parts = []
    for h in range(HEADS):
        oh = o_ref[:, h * HEAD_DIM:(h + 1) * HEAD_DIM]
        ms = jnp.mean(oh * oh, axis=-1, keepdims=True)
        parts.append(oh * lax.rsqrt(ms + EPS) * gon)
    yb = (jnp.concatenate(parts, axis=1) * zb).astype(BF16)
    gate_b = _sigmoid(jnp.dot(hb, wgb_ref[...], preferred_element_type=F32))
    merged = ma_ref[...] + gate_b * jnp.dot(yb, wpb_ref[...], preferred_element_type=F32)
    z = jnp.dot(merged.astype(BF16), wout_ref[...], preferred_element_type=F32)
    zn = z * lax.rsqrt(jnp.mean(z * z, axis=-1, keepdims=True) + EPS) * gpost_ref[...]
    out_ref[...] = x + mod[:, 2 * d:3 * d] * zn


def _merge_out(x2d, mod3, tiles_per_mod, g_pre, g_post, o, m_a, w_pack, w_pb, w_out, g_onorm, tm):
    rows, d = x2d.shape
    const = lambda i: (0, 0)
    row = lambda i: (i, 0)
    return pl.pallas_call(
        _out_kernel,
        grid=(rows // tm,),
        in_specs=[pl.BlockSpec((tm, d), row),
                  pl.BlockSpec((None, 1, 3 * d), lambda i: (i // tiles_per_mod, 0, 0)),
                  pl.BlockSpec((1, d), const),
                  pl.BlockSpec((1, d), const),
                  pl.BlockSpec((tm, d), row),
                  pl.BlockSpec((tm, d), row),
                  pl.BlockSpec((d, d), lambda i: (0, 3)),
                  pl.BlockSpec((d, d), lambda i: (0, 8)),
                  pl.BlockSpec((d, d), const),
                  pl.BlockSpec((d, d), const),
                  pl.BlockSpec((1, HEAD_DIM), const)],
        out_specs=pl.BlockSpec((tm, d), row),
        out_shape=jax.ShapeDtypeStruct((rows, d), F32),
        compiler_params=pltpu.CompilerParams(vmem_limit_bytes=VMEM_LIMIT),
        name="merge_out",
    )(x2d, mod3, g_pre, g_post, o, m_a, w_pack, w_pack, w_pb, w_out, g_onorm)


def _layer(x, ctx, mod_x, mod_c, g_pre, g_post, w_in, w_conv, a_log, dt_bias, g_onorm, gm_ln_g, gm_ln_b,
           w_sp, b_sp, w_pa, w_pb, w_out):
    batch, seq, d = x.shape
    ctx_len = ctx.shape[1]
    kd = HEADS * HEAD_DIM
    off_a = 3 * kd
    off_zb = off_a + 4 * HEADS

    tn = 1024
    w_t = jnp.transpose(w_in)
    w_pack = _repack_weights(w_t, list(range(0, off_a, tn)) + list(range(off_zb, w_in.shape[1], tn)), tn)
    w_ab2 = _repack_logit_weights(w_t, off_a, off_zb - off_a)
    a_row = jnp.pad(a_log.reshape(1, 2 * HEADS), ((0, 0), (0, LANES - 2 * HEADS)))
    dt_row = jnp.pad(dt_bias.reshape(1, 2 * HEADS), ((0, 0), (0, LANES - 2 * HEADS)))
    b_sp_e = jnp.repeat(b_sp.T, HEAD_DIM, axis=1)
    g_pre2 = g_pre.reshape(1, d)

    x2d = x.reshape(batch * seq, d)
    c2d = ctx.reshape(batch * ctx_len, d)
    tm = 1024

    kv_c, ab_c = _project_qkv(c2d, mod_c, 1 << 30, g_pre2, w_pack, w_ab2, w_conv, ctx_len, ctx_len, ("k", "v"))
    s_zero = jnp.zeros((batch, HEADS, HEAD_DIM, HEAD_DIM), F32)
    s_fwd = _gdn_scan(kv_c, ab_c, a_row, dt_row, s_zero, None,
                      batch=batch, reverse=False, with_q=False, tg=ctx_len)
    s_bwd = _gdn_scan(kv_c, ab_c, a_row, dt_row, s_zero, None,
                      batch=batch, reverse=True, with_q=False, tg=ctx_len)

    qkv, ab = _project_qkv(x2d, mod_x, seq // tm, g_pre2, w_pack, w_ab2, w_conv, tm, seq, ("q", "k", "v"))
    tg = 2 * GDN_CHUNK
    o_f = _gdn_scan(qkv, ab, a_row, dt_row, s_fwd, None, batch=batch, reverse=False, with_q=True, tg=tg)
    o = _gdn_scan(qkv, ab, a_row, dt_row, s_bwd, o_f, batch=batch, reverse=True, with_q=True, tg=tg)
    m_a = _branch_a(x2d, mod_x, seq // tm, g_pre2, w_pack, gm_ln_g.reshape(1, d), gm_ln_b.reshape(1, d),
                    w_sp.astype(BF16), b_sp_e, w_pa.astype(BF16), tm)
    out = _merge_out(x2d, mod_x, seq // tm, g_pre2, g_post.reshape(1, d), o, m_a, w_pack,
                     w_pb.astype(BF16), w_out.astype(BF16), g_onorm.reshape(1, HEAD_DIM), tm)
    return out.reshape(batch, seq, d)


def kernel(x, c, ctx, c_ctx, w_mod, b_mod, g_pre, g_post, w_in, w_conv, a_log, dt_bias, g_onorm, gm_ln_g,
           gm_ln_b, w_sp, b_sp, w_pa, w_pb, w_out):
    batch, _, d = x.shape
    depth = w_mod.shape[0]
    assert depth == 1, "context tokens are only read (never updated) by a single-layer stack"
    assert ctx.shape[1] % GDN_CHUNK == 0 and x.shape[1] % (8 * GDN_CHUNK) == 0 and d == D_MODEL
    pad = (-(batch + 1)) % 8
    conds = jnp.concatenate([c, c_ctx[None, :], jnp.zeros((pad, d), F32)], axis=0)
    i = 0
    m = _modulation(conds, w_mod[i], b_mod[i])
    mod_x = m[0:batch].reshape(batch, 1, 3 * d)
    mod_c = m[batch:batch + 1].reshape(1, 1, 3 * d)
    return _layer(x, ctx, mod_x, mod_c, g_pre[i], g_post[i], w_in[i], w_conv[i], a_log[i], dt_bias[i],
                  g_onorm[i], gm_ln_g[i], gm_ln_b[i], w_sp[i], b_sp[i], w_pa[i], w_pb[i], w_out[i])
```

```python
import functools

import jax
import jax.numpy as jnp
from jax import lax
from jax.experimental import pallas as pl
from jax.experimental.pallas import tpu as pltpu

F32 = jnp.float32
BF16 = jnp.bfloat16
HIGHEST = lax.Precision.HIGHEST

D_MODEL = 1024
HEADS = 8
HEAD_DIM = 128
GM_CHUNK = 128
EPS = 1e-6
NEG_BIG = -1e30

LANES = 128
F32_SUBLANES = 8
GDN_CHUNK = 128
BASE_BLOCK = 8
VMEM_LIMIT = 56 * 1024 * 1024


def _sigmoid(x):
    return 0.5 + 0.5 * jnp.tanh(0.5 * x)


def _silu(x):
    h = 0.5 * x
    return h + h * jnp.tanh(h)


def _gelu_tanh(x):
    c = 0.7978845608028654
    return 0.5 * x * (1.0 + jnp.tanh(c * (x + 0.044715 * (x * x * x))))


def _softplus(x):
    return jnp.maximum(x, 0.0) + jnp.log1p(jnp.exp(-jnp.abs(x)))


def _prenorm(x, g_pre, mod):
    ms = jnp.mean(x * x, axis=-1, keepdims=True)
    y = x * lax.rsqrt(ms + EPS) * g_pre
    return y * (1.0 + mod[:, D_MODEL:2 * D_MODEL]) + mod[:, 0:D_MODEL]


def _bdot(a, b):
    return jnp.dot(a.astype(BF16), b.astype(BF16), preferred_element_type=F32)


def _split_bf16(x, parts):
    out, r = [], x
    for _ in range(parts):
        p = r.astype(BF16)
        out.append(p)
        r = r - p.astype(F32)
    return out


def _mod_kernel(c_ref, w_ref, b_ref, o_ref):
    s = _silu(c_ref[...])
    o_ref[...] = jnp.dot(s, w_ref[...], preferred_element_type=F32, precision=HIGHEST) + b_ref[...]


def _modulation(conds, w_mod, b_mod):
    n, d = conds.shape
    cols = w_mod.shape[1]
    tn = 1024
    return pl.pallas_call(
        _mod_kernel,
        grid=(cols // tn,),
        in_specs=[pl.BlockSpec((n, d), lambda j: (0, 0)),
                  pl.BlockSpec((d, tn), lambda j: (0, j)),
                  pl.BlockSpec((1, tn), lambda j: (0, j))],
        out_specs=pl.BlockSpec((n, tn), lambda j: (0, j)),
        out_shape=jax.ShapeDtypeStruct((n, cols), F32),
        name="mod",
    )(conds, w_mod, b_mod.reshape(1, cols))


def _repack_kernel(tbl_ref, w_ref, o_ref):
    del tbl_ref
    o_ref[...] = w_ref[...].T.astype(BF16)


def _repack_weights(w_t, src_rows, tn):
    _, d = w_t.shape
    nblk = len(src_rows)
    grid_spec = pltpu.PrefetchScalarGridSpec(
        num_scalar_prefetch=1,
        grid=(nblk,),
        in_specs=[pl.BlockSpec((pl.Element(tn), pl.Element(d)),
                               lambda j, tbl: (pl.multiple_of(tbl[j], F32_SUBLANES), 0))],
        out_specs=pl.BlockSpec((d, tn), lambda j, tbl: (0, j)),
    )
    return pl.pallas_call(
        _repack_kernel,
        grid_spec=grid_spec,
        out_shape=jax.ShapeDtypeStruct((d, nblk * tn), BF16),
        name="repack",
    )(jnp.asarray(src_rows, jnp.int32), w_t)


def _repack_logits_kernel(w_ref, o_ref, *, n_cols):
    wt = w_ref[...].T
    lane = lax.broadcasted_iota(jnp.int32, wt.shape, 1)
    hi, lo = _split_bf16(jnp.where(lane < n_cols, wt, 0.0), 2)
    o_ref[:, 0:LANES] = hi
    o_ref[:, LANES:2 * LANES] = lo


def _repack_logit_weights(w_t, row0, n_cols):
    _, d = w_t.shape
    return pl.pallas_call(
        functools.partial(_repack_logits_kernel, n_cols=n_cols),
        grid=(1,),
        in_specs=[pl.BlockSpec((pl.Element(LANES), pl.Element(d)), lambda j: (row0, 0))],
        out_specs=pl.BlockSpec((d, 2 * LANES), lambda j: (0, 0)),
        out_shape=jax.ShapeDtypeStruct((d, 2 * LANES), BF16),
        name="repack_logits",
    )(w_t)


def _proj_kernel(x_ref, xp_ref, xn_ref, mod_ref, gpre_ref, w_ref, wab_ref, wconv_ref, y_ref, ab_ref, *,
                 kinds, col0, tiles_per_seq):
    tm = x_ref.shape[0]
    halo = xp_ref.shape[0]
    pos = lax.rem(pl.program_id(0), tiles_per_seq)
    xa = jnp.concatenate([xp_ref[...], x_ref[...], xn_ref[...]], axis=0)
    h = _prenorm(xa, gpre_ref[...], mod_ref[...])
    hb = h.astype(BF16)
    rows = lax.broadcasted_iota(jnp.int32, (tm, 1), 0)
    keep_dn = (rows != 0) | (pos != 0)
    keep_up = (rows != tm - 1) | (pos != tiles_per_seq - 1)
    for n, kind in enumerate(kinds):
        sl = slice(n * D_MODEL, (n + 1) * D_MODEL)
        wsl = slice(col0 + n * D_MODEL, col0 + (n + 1) * D_MODEL)
        p = jnp.dot(hb, w_ref[:, wsl], preferred_element_type=F32)
        wc = wconv_ref[:, wsl]
        p_dn = jnp.where(keep_dn, pltpu.roll(p, 1, axis=0)[halo:halo + tm], 0.0)
        p_up = jnp.where(keep_up, pltpu.roll(p, tm + 2 * halo - 1, axis=0)[halo:halo + tm], 0.0)
        y = _silu(p_dn * wc[0:1, :] + p[halo:halo + tm] * wc[1:2, :] + p_up * wc[2:3, :])
        if kind == "v":
            y_ref[:, sl] = y.astype(BF16)
            continue
        scale = HEAD_DIM ** -0.5 if kind == "q" else 1.0
        yhs = [y[:, hd * HEAD_DIM:(hd + 1) * HEAD_DIM] for hd in range(HEADS)]
        ss = jnp.concatenate([jnp.sum(yh * yh, axis=-1, keepdims=True) for yh in yhs], axis=1)
        inv = lax.rsqrt(ss + EPS) * scale
        for hd in range(HEADS):
            y_ref[:, n * D_MODEL + hd * HEAD_DIM:n * D_MODEL + (hd + 1) * HEAD_DIM] = (
                yhs[hd] * inv[:, hd:hd + 1]).astype(BF16)
    h_hi, h_lo = _split_bf16(h[halo:halo + tm], 2)
    hw = jnp.dot(h_hi, wab_ref[...], preferred_element_type=F32)
    ab_ref[...] = (hw[:, 0:LANES] + hw[:, LANES:2 * LANES]
                   + jnp.dot(h_lo, wab_ref[:, 0:LANES], preferred_element_type=F32))


def _project_qkv(x2d, mod3, tiles_per_mod, g_pre, w_pack, w_ab, w_conv, tm, seq, kinds):
    rows, d = x2d.shape
    nq = len(kinds) * d
    nw = 3 * d
    halo = F32_SUBLANES
    hb = tm // halo
    last_halo = rows // halo - 1
    kern = functools.partial(_proj_kernel, kinds=kinds, col0=nw - nq, tiles_per_seq=seq // tm)
    return pl.pallas_call(
        kern,
        grid=(rows // tm,),
        in_specs=[pl.BlockSpec((tm, d), lambda i: (i, 0)),
                  pl.BlockSpec((halo, d), lambda i: (jnp.maximum(i * hb - 1, 0), 0)),
                  pl.BlockSpec((halo, d), lambda i: (jnp.minimum((i + 1) * hb, last_halo), 0)),
                  pl.BlockSpec((None, 1, 3 * d), lambda i: (i // tiles_per_mod, 0, 0)),
                  pl.BlockSpec((1, d), lambda i: (0, 0)),
                  pl.BlockSpec((d, nw), lambda i: (0, 0)),
                  pl.BlockSpec((d, 2 * LANES), lambda i: (0, 0)),
                  pl.BlockSpec((3, nw), lambda i: (0, 0))],
        out_specs=[pl.BlockSpec((tm, nq), lambda i: (i, 0)),
                   pl.BlockSpec((tm, LANES), lambda i: (i, 0))],
        out_shape=[jax.ShapeDtypeStruct((rows, nq), BF16),
                   jax.ShapeDtypeStruct((rows, LANES), F32)],
        compiler_params=pltpu.CompilerParams(vmem_limit_bytes=VMEM_LIMIT),
        name="proj",
    )(x2d, x2d, x2d, mod3, g_pre, w_pack, w_ab, w_conv)


def _unit_tri_inverse(a_list, eye, ii, jj):
    c = a_list[0].shape[0]
    shift = BASE_BLOCK.bit_length() - 1
    same = (ii >> shift) == (jj >> shift)
    d1f = [jnp.where(same, a, 0.0) for a in a_list]
    d1 = [d.astype(BF16) for d in d1f]
    d2 = [jnp.dot(d, d, preferred_element_type=F32) for d in d1]
    d2b = [d.astype(BF16) for d in d2]
    d34 = [jnp.dot(jnp.concatenate([x, y], axis=0), y, preferred_element_type=F32)
           for x, y in zip(d1, d2b)]
    z = [eye - x + y - p[0:c] for x, y, p in zip(d1f, d2, d34)]
    t = [x + _bdot(x, p[c:2 * c]) for x, p in zip(z, d34)]
    b = BASE_BLOCK
    while b < c:
        s1 = b.bit_length() - 1
        off = ((ii >> (s1 + 1)) == (jj >> (s1 + 1))) & ((ii >> s1) != (jj >> s1))
        tb = [x.astype(BF16) for x in t]
        tc = [jnp.dot(x, jnp.where(off, a, 0.0).astype(BF16), preferred_element_type=F32)
              for x, a in zip(tb, a_list)]
        t = [x - jnp.dot(y.astype(BF16), xb, preferred_element_type=F32) for x, y, xb in zip(t, tc, tb)]
        b *= 2
    return t


def _gdn_kernel(*refs, reverse, with_q, accumulate, n_t, tg):
    it = iter(refs)
    y_ref, ab_ref, arow_ref, dtrow_ref, s0_ref = next(it), next(it), next(it), next(it), next(it)
    oin_ref = next(it) if accumulate else None
    o_ref = next(it) if with_q else None
    sfin_ref = None if with_q else next(it)
    s_scr = next(it)

    c = GDN_CHUNK
    i = pl.program_id(1)
    ncol = y_ref.shape[1]
    koff = ncol - 2 * D_MODEL
    voff = ncol - D_MODEL

    @pl.when(i == 0)
    def _():
        s_scr[...] = s0_ref[...]

    ab = ab_ref[...]
    g_cmp = -jnp.exp(arow_ref[...]) * _softplus(ab + dtrow_ref[...])
    beta_cmp = _sigmoid(ab)
    ti = lax.broadcasted_iota(jnp.int32, (tg, tg), 0)
    tj = lax.broadcasted_iota(jnp.int32, (tg, tg), 1)
    cs = c.bit_length() - 1
    same_chunk = (ti >> cs) == (tj >> cs)
    order = (tj >= ti) if reverse else (tj <= ti)
    cum_mat = jnp.where(same_chunk & order, 1.0, 0.0).astype(BF16)
    gcum3 = jnp.dot(cum_mat, jnp.concatenate(_split_bf16(g_cmp, 3), axis=1), preferred_element_type=F32)
    gcum_cmp = gcum3[:, 0:LANES] + gcum3[:, LANES:2 * LANES] + gcum3[:, 2 * LANES:3 * LANES]
    gcum_t = gcum_cmp.T
    eg_cmp = jnp.exp(gcum_cmp)
    dir_off = HEADS if reverse else 0

    ii = lax.broadcasted_iota(jnp.int32, (c, c), 0)
    jj = lax.broadcasted_iota(jnp.int32, (c, c), 1)
    eye = jnp.where(ii == jj, 1.0, 0.0).astype(F32)
    incl = (ii <= jj) if reverse else (ii >= jj)
    strict = (ii < jj) if reverse else (ii > jj)

    n_chunks = tg // c
    chunk_order = list(range(n_chunks - 1, -1, -1) if reverse else range(n_chunks))
    inst = [(cidx, h) for cidx in chunk_order for h in range(HEADS)]

    knb, qsb, be, gc, eg, decay = [], [], [], [], [], []
    for cidx, h in inst:
        rs = slice(cidx * c, (cidx + 1) * c)
        knb.append(y_ref[rs, koff + h * HEAD_DIM:koff + (h + 1) * HEAD_DIM])
        if with_q:
            qsb.append(y_ref[rs, h * HEAD_DIM:(h + 1) * HEAD_DIM])
        la = dir_off + h
        lb = 2 * HEADS + dir_off + h
        be.append(jnp.broadcast_to(beta_cmp[rs, lb:lb + 1], (c, HEAD_DIM)))
        g_col = jnp.broadcast_to(gcum_cmp[rs, la:la + 1], (c, HEAD_DIM))
        g_row = gcum_t[la:la + 1, cidx * c:(cidx + 1) * c]
        gc.append(g_col)
        decay.append(jnp.exp(jnp.where(incl, g_col - g_row, NEG_BIG)))
        eg.append(jnp.broadcast_to(eg_cmp[rs, la:la + 1], (c, HEAD_DIM)))
    kn = [k.astype(F32) for k in knb]
    kb = [k * b for k, b in zip(kn, be)]
    nt_dims = (((1,), (1,)), ((), ()))
    if with_q:
        qs = [q.astype(F32) for q in qsb]
        kk = [lax.dot_general(jnp.concatenate([a.astype(BF16), q], axis=0), k, nt_dims,
                              preferred_element_type=F32) for a, q, k in zip(kb, qsb, knb)]
        attn = [(x[c:2 * c] * d).astype(BF16) for x, d in zip(kk, decay)]
    else:
        kk = [lax.dot_general(a.astype(BF16), k, nt_dims, preferred_element_type=F32)
              for a, k in zip(kb, knb)]
    a_mat = [jnp.where(strict, x[0:c] * d, 0.0) for x, d in zip(kk, decay)]
    t_inv = _unit_tri_inverse(a_mat, eye, ii, jj)
    sol = []
    for n, (cidx, h) in enumerate(inst):
        rs = slice(cidx * c, (cidx + 1) * c)
        vv = y_ref[rs, voff + h * HEAD_DIM:voff + (h + 1) * HEAD_DIM].astype(F32)
        rhs = jnp.concatenate([vv * be[n], kb[n] * eg[n]], axis=1)
        sol.append(_bdot(t_inv[n], rhs))

    for ci in range(n_chunks):
        base = ci * HEADS
        idx = range(base, base + HEADS)
        rs = slice(chunk_order[ci] * c, (chunk_order[ci] + 1) * c)
        s_old = [s_scr[h] for h in range(HEADS)]
        s_b = [s.astype(BF16) for s in s_old]
        if with_q:
            ws = [jnp.dot(jnp.concatenate([sol[n][:, HEAD_DIM:], qs[n] * eg[n]], axis=0).astype(BF16), sb,
                          preferred_element_type=F32) for n, sb in zip(idx, s_b)]
        else:
            ws = [jnp.dot(sol[n][:, HEAD_DIM:].astype(BF16), sb, preferred_element_type=F32)
                  for n, sb in zip(idx, s_b)]
        v_new = [(sol[n][:, 0:HEAD_DIM] - w[0:c]).astype(BF16) for n, w in zip(idx, ws)]
        last = rs.start if reverse else rs.stop - 1
        gl_cmp = gcum_cmp[last:last + 1, :]
        tail_cmp = jnp.exp(gl_cmp - gcum_cmp[rs, :])
        egl_cmp = jnp.exp(gl_cmp)
        k_tail = [(kn[n] * jnp.broadcast_to(tail_cmp[:, dir_off + h:dir_off + h + 1], (c, HEAD_DIM))).T
                  .astype(BF16) for h, n in enumerate(idx)]
        if with_q:
            kva = [jnp.dot(jnp.concatenate([kt, attn[n]], axis=0), v, preferred_element_type=F32)
                   for kt, n, v in zip(k_tail, idx, v_new)]
        else:
            kva = [jnp.dot(kt, v, preferred_element_type=F32) for kt, v in zip(k_tail, v_new)]
        for h in range(HEADS):
            s_decay = jnp.broadcast_to(egl_cmp[:, dir_off + h:dir_off + h + 1], (HEAD_DIM, HEAD_DIM))
            s_scr[h] = s_old[h] * s_decay + kva[h][0:HEAD_DIM]
        if with_q:
            for h in range(HEADS):
                hs = slice(h * HEAD_DIM, (h + 1) * HEAD_DIM)
                o = ws[h][c:2 * c] + kva[h][HEAD_DIM:HEAD_DIM + c]
                if accumulate:
                    o = o + oin_ref[rs, hs]
                o_ref[rs, hs] = o

    if not with_q:
        @pl.when(i == n_t - 1)
        def _():
            sfin_ref[...] = s_scr[...]


def _gdn_scan(y, ab, a_row, dt_row, s0, o_in, *, batch, reverse, with_q, tg):
    rows, ncol = y.shape
    n_t = rows // batch // tg
    accumulate = o_in is not None

    def tile(b, i):
        return (b * n_t + ((n_t - 1 - i) if reverse else i), 0)

    state = lambda b, i: (b, 0, 0, 0)
    const = lambda b, i: (0, 0)
    in_specs = [
        pl.BlockSpec((tg, ncol), tile),
        pl.BlockSpec((tg, LANES), tile),
        pl.BlockSpec((1, LANES), const),
        pl.BlockSpec((1, LANES), const),
        pl.BlockSpec((None, HEADS, HEAD_DIM, HEAD_DIM), state),
    ]
    args = [y, ab, a_row, dt_row, s0]
    if accumulate:
        in_specs.append(pl.BlockSpec((tg, D_MODEL), tile))
        args.append(o_in)
    if with_q:
        out_specs = pl.BlockSpec((tg, D_MODEL), tile)
        out_shape = jax.ShapeDtypeStruct((rows, D_MODEL), F32)
    else:
        out_specs = pl.BlockSpec((None, HEADS, HEAD_DIM, HEAD_DIM), state)
        out_shape = jax.ShapeDtypeStruct((batch, HEADS, HEAD_DIM, HEAD_DIM), F32)
    kern = functools.partial(_gdn_kernel, reverse=reverse, with_q=with_q, accumulate=accumulate,
                             n_t=n_t, tg=tg)
    return pl.pallas_call(
        kern,
        grid=(batch, n_t),
        in_specs=in_specs,
        out_specs=out_specs,
        out_shape=out_shape,
        scratch_shapes=[pltpu.VMEM((HEADS, HEAD_DIM, HEAD_DIM), F32)],
        compiler_params=pltpu.CompilerParams(dimension_semantics=("arbitrary", "arbitrary"),
                                             vmem_limit_bytes=VMEM_LIMIT),
        name="gdn_bwd" if reverse else "gdn_fwd",
    )(*args)


def _brancha_kernel(x_ref, mod_ref, gpre_ref, wa_ref, lng_ref, lnb_ref, wsp_ref, bsp_ref, wpa_ref, ma_ref):
    d = D_MODEL
    tm = x_ref.shape[0]
    hb = _prenorm(x_ref[...], gpre_ref[...], mod_ref[...]).astype(BF16)
    vg = _gelu_tanh(jnp.dot(hb, wa_ref[:, d:2 * d], preferred_element_type=F32))
    mu = jnp.mean(vg, axis=-1, keepdims=True)
    vc = vg - mu
    var = jnp.mean(vc * vc, axis=-1, keepdims=True)
    v = (vc * lax.rsqrt(var + EPS) * lng_ref[...] + lnb_ref[...]).astype(BF16)
    parts = []
    for n in range(tm // GM_CHUNK):
        rs = slice(n * GM_CHUNK, (n + 1) * GM_CHUNK)
        cols = [jnp.dot(wsp_ref[g], v[rs, g * HEAD_DIM:(g + 1) * HEAD_DIM], preferred_element_type=F32)
                for g in range(d // HEAD_DIM)]
        parts.append(jnp.concatenate(cols, axis=1) + bsp_ref[...])
    s = jnp.concatenate(parts, axis=0)
    u = _gelu_tanh(jnp.dot(hb, wa_ref[:, 0:d], preferred_element_type=F32))
    z = _silu(jnp.dot(hb, wa_ref[:, 2 * d:3 * d], preferred_element_type=F32))
    ya = (u * s * z).astype(BF16)
    gate = _sigmoid(jnp.dot(hb, wa_ref[:, 3 * d:4 * d], preferred_element_type=F32))
    ma_ref[...] = gate * jnp.dot(ya, wpa_ref[...], preferred_element_type=F32)


def _branch_a(x2d, mod3, tiles_per_mod, g_pre, w_pack, ln_g, ln_b, w_sp, b_sp_e, w_pa, tm):
    rows, d = x2d.shape
    const = lambda i: (0, 0)
    return pl.pallas_call(
        _brancha_kernel,
        grid=(rows // tm,),
        in_specs=[pl.BlockSpec((tm, d), lambda i: (i, 0)),
                  pl.BlockSpec((None, 1, 3 * d), lambda i: (i // tiles_per_mod, 0, 0)),
                  pl.BlockSpec((1, d), const),
                  pl.BlockSpec((d, 4 * d), lambda i: (0, 1)),
                  pl.BlockSpec((1, d), const),
                  pl.BlockSpec((1, d), const),
                  pl.BlockSpec(w_sp.shape, lambda i: (0, 0, 0)),
                  pl.BlockSpec((GM_CHUNK, d), const),
                  pl.BlockSpec((d, d), const)],
        out_specs=pl.BlockSpec((tm, d), lambda i: (i, 0)),
        out_shape=jax.ShapeDtypeStruct((rows, d), F32),
        compiler_params=pltpu.CompilerParams(vmem_limit_bytes=VMEM_LIMIT),
        name="brancha",
    )(x2d, mod3, g_pre, w_pack, ln_g, ln_b, w_sp, b_sp_e, w_pa)


def _out_kernel(x_ref, mod_ref, gpre_ref, gpost_ref, o_ref, ma_ref, wzb_ref, wgb_ref, wpb_ref, wout_ref,
                gon_ref, out_ref):
    d = D_MODEL
    x = x_ref[...]
    mod = mod_ref[...]
    hb = _prenorm(x, gpre_ref[...], mod).astype(BF16)
    zb = _silu(jnp.dot(hb, wzb_ref[...], preferred_element_type=F32))
    gon = gon_ref[...]
    ohs = [o_ref[:, h * HEAD_DIM:(h + 1) * HEAD_DIM] for h in range(HEADS)]
    ms = jnp.concatenate([jnp.mean(oh * oh, axis=-1, keepdims=True) for oh in ohs], axis=1)
    inv = lax.rsqrt(ms + EPS)
    parts = [oh * inv[:, h:h + 1] * gon for h, oh in enumerate(ohs)]
    yb = (jnp.concatenate(parts, axis=1) * zb).astype(BF16)
    gate_b = _sigmoid(jnp.dot(hb, wgb_ref[...], preferred_element_type=F32))
    merged = ma_ref[...] + gate_b * jnp.dot(yb, wpb_ref[...], preferred_element_type=F32)
    z = jnp.dot(merged.astype(BF16), wout_ref[...], preferred_element_type=F32)
    zn = z * lax.rsqrt(jnp.mean(z * z, axis=-1, keepdims=True) + EPS) * gpost_ref[...]
    out_ref[...] = x + mod[:, 2 * d:3 * d] * zn


def _merge_out(x2d, mod3, tiles_per_mod, g_pre, g_post, o, m_a, w_pack, w_pb, w_out, g_onorm, tm):
    rows, d = x2d.shape
    const = lambda i: (0, 0)
    row = lambda i: (i, 0)
    return pl.pallas_call(
        _out_kernel,
        grid=(rows // tm,),
        in_specs=[pl.BlockSpec((tm, d), row),
                  pl.BlockSpec((None, 1, 3 * d), lambda i: (i // tiles_per_mod, 0, 0)),
                  pl.BlockSpec((1, d), const),
                  pl.BlockSpec((1, d), const),
                  pl.BlockSpec((tm, d), row),
                  pl.BlockSpec((tm, d), row),
                  pl.BlockSpec((d, d), lambda i: (0, 3)),
                  pl.BlockSpec((d, d), lambda i: (0, 8)),
                  pl.BlockSpec((d, d), const),
                  pl.BlockSpec((d, d), const),
                  pl.BlockSpec((1, HEAD_DIM), const)],
        out_specs=pl.BlockSpec((tm, d), row),
        out_shape=jax.ShapeDtypeStruct((rows, d), F32),
        compiler_params=pltpu.CompilerParams(vmem_limit_bytes=VMEM_LIMIT),
        name="merge_out",
    )(x2d, mod3, g_pre, g_post, o, m_a, w_pack, w_pack, w_pb, w_out, g_onorm)


def _layer(x, ctx, mod_x, mod_c, g_pre, g_post, w_in, w_conv, a_log, dt_bias, g_onorm, gm_ln_g, gm_ln_b,
           w_sp, b_sp, w_pa, w_pb, w_out):
    batch, seq, d = x.shape
    ctx_len = ctx.shape[1]
    kd = HEADS * HEAD_DIM
    off_a = 3 * kd
    off_zb = off_a + 4 * HEADS

    tn = 1024
    w_t = jnp.transpose(w_in)
    w_pack = _repack_weights(w_t, list(range(0, off_a, tn)) + list(range(off_zb, w_in.shape[1], tn)), tn)
    w_ab2 = _repack_logit_weights(w_t, off_a, off_zb - off_a)
    a_row = jnp.pad(a_log.reshape(1, 2 * HEADS), ((0, 0), (0, LANES - 2 * HEADS)))
    dt_row = jnp.pad(dt_bias.reshape(1, 2 * HEADS), ((0, 0), (0, LANES - 2 * HEADS)))
    b_sp_e = jnp.repeat(b_sp.T, HEAD_DIM, axis=1)
    g_pre2 = g_pre.reshape(1, d)

    x2d = x.reshape(batch * seq, d)
    c2d = ctx.reshape(batch * ctx_len, d)
    tm = 1024

    kv_c, ab_c = _project_qkv(c2d, mod_c, 1 << 30, g_pre2, w_pack, w_ab2, w_conv, ctx_len, ctx_len, ("k", "v"))
    s_zero = jnp.zeros((batch, HEADS, HEAD_DIM, HEAD_DIM), F32)
    s_fwd = _gdn_scan(kv_c, ab_c, a_row, dt_row, s_zero, None,
                      batch=batch, reverse=False, with_q=False, tg=ctx_len)
    s_bwd = _gdn_scan(kv_c, ab_c, a_row, dt_row, s_zero, None,
                      batch=batch, reverse=True, with_q=False, tg=ctx_len)

    qkv, ab = _project_qkv(x2d, mod_x, seq // tm, g_pre2, w_pack, w_ab2, w_conv, tm, seq, ("q", "k", "v"))
    tg = 2 * GDN_CHUNK
    o_f = _gdn_scan(qkv, ab, a_row, dt_row, s_fwd, None, batch=batch, reverse=False, with_q=True, tg=tg)
    o = _gdn_scan(qkv, ab, a_row, dt_row, s_bwd, o_f, batch=batch, reverse=True, with_q=True, tg=tg)
    m_a = _branch_a(x2d, mod_x, seq // tm, g_pre2, w_pack, gm_ln_g.reshape(1, d), gm_ln_b.reshape(1, d),
                    w_sp.astype(BF16), b_sp_e, w_pa.astype(BF16), tm)
    out = _merge_out(x2d, mod_x, seq // tm, g_pre2, g_post.reshape(1, d), o, m_a, w_pack,
                     w_pb.astype(BF16), w_out.astype(BF16), g_onorm.reshape(1, HEAD_DIM), tm)
    return out.reshape(batch, seq, d)


def kernel(x, c, ctx, c_ctx, w_mod, b_mod, g_pre, g_post, w_in, w_conv, a_log, dt_bias, g_onorm, gm_ln_g,
           gm_ln_b, w_sp, b_sp, w_pa, w_pb, w_out):
    batch, _, d = x.shape
    depth = w_mod.shape[0]
    assert depth == 1, "context tokens are only read (never updated) by a single-layer stack"
    assert ctx.shape[1] % GDN_CHUNK == 0 and x.shape[1] % (8 * GDN_CHUNK) == 0 and d == D_MODEL
    pad = (-(batch + 1)) % 8
    conds = jnp.concatenate([c, c_ctx[None, :], jnp.zeros((pad, d), F32)], axis=0)
    i = 0
    m = _modulation(conds, w_mod[i], b_mod[i])
    mod_x = m[0:batch].reshape(batch, 1, 3 * d)
    mod_c = m[batch:batch + 1].reshape(1, 1, 3 * d)
    return _layer(x, ctx, mod_x, mod_c, g_pre[i], g_post[i], w_in[i], w_conv[i], a_log[i], dt_bias[i],
                  g_onorm[i], gm_ln_g[i], gm_ln_b[i], w_sp[i], b_sp[i], w_pa[i], w_pb[i], w_out[i])
```

```python
import functools

import jax
import jax.numpy as jnp
from jax import lax
from jax.experimental import pallas as pl
from jax.experimental.pallas import tpu as pltpu

F32 = jnp.float32
BF16 = jnp.bfloat16
HIGHEST = lax.Precision.HIGHEST

D_MODEL = 1024
HEADS = 8
HEAD_DIM = 128
GM_CHUNK = 128
EPS = 1e-6
NEG_BIG = -1e30

LANES = 128
F32_SUBLANES = 8
GDN_CHUNK = 128
BASE_BLOCK = 8
VMEM_LIMIT = 56 * 1024 * 1024


def _sigmoid(x):
    return 0.5 + 0.5 * jnp.tanh(0.5 * x)


def _silu(x):
    h = 0.5 * x
    return h + h * jnp.tanh(h)


def _gelu_tanh(x):
    c = 0.7978845608028654
    return 0.5 * x * (1.0 + jnp.tanh(c * (x + 0.044715 * (x * x * x))))


def _softplus(x):
    return jnp.maximum(x, 0.0) + jnp.log1p(jnp.exp(-jnp.abs(x)))


def _prenorm(x, g_pre, mod):
    ms = jnp.mean(x * x, axis=-1, keepdims=True)
    y = x * lax.rsqrt(ms + EPS) * g_pre
    return y * (1.0 + mod[:, D_MODEL:2 * D_MODEL]) + mod[:, 0:D_MODEL]


def _bdot(a, b):
    return jnp.dot(a.astype(BF16), b.astype(BF16), preferred_element_type=F32)


def _split_bf16(x, parts):
    out, r = [], x
    for _ in range(parts):
        p = r.astype(BF16)
        out.append(p)
        r = r - p.astype(F32)
    return out


def _mod_kernel(c_ref, w_ref, b_ref, o_ref):
    s = _silu(c_ref[...])
    o_ref[...] = jnp.dot(s, w_ref[...], preferred_element_type=F32, precision=HIGHEST) + b_ref[...]


def _modulation(conds, w_mod, b_mod):
    n, d = conds.shape
    cols = w_mod.shape[1]
    tn = 1024
    return pl.pallas_call(
        _mod_kernel,
        grid=(cols // tn,),
        in_specs=[pl.BlockSpec((n, d), lambda j: (0, 0)),
                  pl.BlockSpec((d, tn), lambda j: (0, j)),
                  pl.BlockSpec((1, tn), lambda j: (0, j))],
        out_specs=pl.BlockSpec((n, tn), lambda j: (0, j)),
        out_shape=jax.ShapeDtypeStruct((n, cols), F32),
        name="mod",
    )(conds, w_mod, b_mod.reshape(1, cols))


def _repack_kernel(tbl_ref, w_ref, o_ref):
    del tbl_ref
    o_ref[...] = w_ref[...].T.astype(BF16)


def _repack_weights(w_t, src_rows, tn):
    _, d = w_t.shape
    nblk = len(src_rows)
    grid_spec = pltpu.PrefetchScalarGridSpec(
        num_scalar_prefetch=1,
        grid=(nblk,),
        in_specs=[pl.BlockSpec((pl.Element(tn), pl.Element(d)),
                               lambda j, tbl: (pl.multiple_of(tbl[j], F32_SUBLANES), 0))],
        out_specs=pl.BlockSpec((d, tn), lambda j, tbl: (0, j)),
    )
    return pl.pallas_call(
        _repack_kernel,
        grid_spec=grid_spec,
        out_shape=jax.ShapeDtypeStruct((d, nblk * tn), BF16),
        name="repack",
    )(jnp.asarray(src_rows, jnp.int32), w_t)


def _repack_logits_kernel(w_ref, o_ref, *, n_cols):
    wt = w_ref[...].T
    lane = lax.broadcasted_iota(jnp.int32, wt.shape, 1)
    hi, lo = _split_bf16(jnp.where(lane < n_cols, wt, 0.0), 2)
    o_ref[:, 0:LANES] = hi
    o_ref[:, LANES:2 * LANES] = lo


def _repack_logit_weights(w_t, row0, n_cols):
    _, d = w_t.shape
    return pl.pallas_call(
        functools.partial(_repack_logits_kernel, n_cols=n_cols),
        grid=(1,),
        in_specs=[pl.BlockSpec((pl.Element(LANES), pl.Element(d)), lambda j: (row0, 0))],
        out_specs=pl.BlockSpec((d, 2 * LANES), lambda j: (0, 0)),
        out_shape=jax.ShapeDtypeStruct((d, 2 * LANES), BF16),
        name="repack_logits",
    )(w_t)


def _proj_kernel(x_ref, xp_ref, xn_ref, mod_ref, gpre_ref, w_ref, wab_ref, wconv_ref, y_ref, ab_ref, *,
                 kinds, col0, tiles_per_seq):
    tm = x_ref.shape[0]
    halo = xp_ref.shape[0]
    pos = lax.rem(pl.program_id(0), tiles_per_seq)
    xa = jnp.concatenate([xp_ref[...], x_ref[...], xn_ref[...]], axis=0)
    h = _prenorm(xa, gpre_ref[...], mod_ref[...])
    hb = h.astype(BF16)
    rows = lax.broadcasted_iota(jnp.int32, (tm, 1), 0)
    keep_dn = (rows != 0) | (pos != 0)
    keep_up = (rows != tm - 1) | (pos != tiles_per_seq - 1)
    for n, kind in enumerate(kinds):
        sl = slice(n * D_MODEL, (n + 1) * D_MODEL)
        wsl = slice(col0 + n * D_MODEL, col0 + (n + 1) * D_MODEL)
        p = jnp.dot(hb, w_ref[:, wsl], preferred_element_type=F32)
        wc = wconv_ref[:, wsl]
        p_dn = jnp.where(keep_dn, pltpu.roll(p, 1, axis=0)[halo:halo + tm], 0.0)
        p_up = jnp.where(keep_up, pltpu.roll(p, tm + 2 * halo - 1, axis=0)[halo:halo + tm], 0.0)
        y = _silu(p_dn * wc[0:1, :] + p[halo:halo + tm] * wc[1:2, :] + p_up * wc[2:3, :])
        if kind == "v":
            y_ref[:, sl] = y.astype(BF16)
            continue
        scale = HEAD_DIM ** -0.5 if kind == "q" else 1.0
        yhs = [y[:, hd * HEAD_DIM:(hd + 1) * HEAD_DIM] for hd in range(HEADS)]
        ss = jnp.concatenate([jnp.sum(yh * yh, axis=-1, keepdims=True) for yh in yhs], axis=1)
        inv = lax.rsqrt(ss + EPS) * scale
        for hd in range(HEADS):
            y_ref[:, n * D_MODEL + hd * HEAD_DIM:n * D_MODEL + (hd + 1) * HEAD_DIM] = (
                yhs[hd] * inv[:, hd:hd + 1]).astype(BF16)
    h_hi, h_lo = _split_bf16(h[halo:halo + tm], 2)
    hw = jnp.dot(h_hi, wab_ref[...], preferred_element_type=F32)
    ab_ref[...] = (hw[:, 0:LANES] + hw[:, LANES:2 * LANES]
                   + jnp.dot(h_lo, wab_ref[:, 0:LANES], preferred_element_type=F32))


def _project_qkv(x2d, mod3, tiles_per_mod, g_pre, w_pack, w_ab, w_conv, tm, seq, kinds):
    rows, d = x2d.shape
    nq = len(kinds) * d
    nw = 3 * d
    halo = F32_SUBLANES
    hb = tm // halo
    last_halo = rows // halo - 1
    kern = functools.partial(_proj_kernel, kinds=kinds, col0=nw - nq, tiles_per_seq=seq // tm)
    return pl.pallas_call(
        kern,
        grid=(rows // tm,),
        in_specs=[pl.BlockSpec((tm, d), lambda i: (i, 0)),
                  pl.BlockSpec((halo, d), lambda i: (jnp.maximum(i * hb - 1, 0), 0)),
                  pl.BlockSpec((halo, d), lambda i: (jnp.minimum((i + 1) * hb, last_halo), 0)),
                  pl.BlockSpec((None, 1, 3 * d), lambda i: (i // tiles_per_mod, 0, 0)),
                  pl.BlockSpec((1, d), lambda i: (0, 0)),
                  pl.BlockSpec((d, nw), lambda i: (0, 0)),
                  pl.BlockSpec((d, 2 * LANES), lambda i: (0, 0)),
                  pl.BlockSpec((3, nw), lambda i: (0, 0))],
        out_specs=[pl.BlockSpec((tm, nq), lambda i: (i, 0)),
                   pl.BlockSpec((tm, LANES), lambda i: (i, 0))],
        out_shape=[jax.ShapeDtypeStruct((rows, nq), BF16),
                   jax.ShapeDtypeStruct((rows, LANES), F32)],
        compiler_params=pltpu.CompilerParams(vmem_limit_bytes=VMEM_LIMIT),
        name="proj",
    )(x2d, x2d, x2d, mod3, g_pre, w_pack, w_ab, w_conv)


def _unit_tri_inverse(a_list, eye, ii, jj, upper):
    c = a_list[0].shape[0]
    shift = BASE_BLOCK.bit_length() - 1
    same = (ii >> shift) == (jj >> shift)
    d1f = [jnp.where(same, a, 0.0) for a in a_list]
    d1 = [d.astype(BF16) for d in d1f]
    d2 = [jnp.dot(d, d, preferred_element_type=F32) for d in d1]
    d2b = [d.astype(BF16) for d in d2]
    d34 = [jnp.dot(jnp.concatenate([x, y], axis=0), y, preferred_element_type=F32)
           for x, y in zip(d1, d2b)]
    z = [eye - x + y - p[0:c] for x, y, p in zip(d1f, d2, d34)]
    t = [x + _bdot(x, p[c:2 * c]) for x, p in zip(z, d34)]
    b = BASE_BLOCK
    while b < c:
        s1 = b.bit_length() - 1
        off = ((ii >> (s1 + 1)) == (jj >> (s1 + 1))) & ((ii >> s1) != (jj >> s1))
        upd = [(r, r + b) for r in range(0 if upper else b, c, 2 * b)]
        rows = [jnp.concatenate([x[lo:hi] for lo, hi in upd], axis=0) for x in t]
        tc = [jnp.dot(r.astype(BF16), jnp.where(off, a, 0.0).astype(BF16), preferred_element_type=F32)
              for r, a in zip(rows, a_list)]
        new_rows = [r - jnp.dot(y.astype(BF16), x.astype(BF16), preferred_element_type=F32)
                    for r, y, x in zip(rows, tc, t)]
        t = [jnp.concatenate([nr[upd.index((r, r + b)) * b:(upd.index((r, r + b)) + 1) * b]
                              if (r, r + b) in upd else x[r:r + b] for r in range(0, c, b)], axis=0)
             for x, nr in zip(t, new_rows)]
        b *= 2
    return t


def _gdn_kernel(*refs, reverse, with_q, accumulate, n_t, tg):
    it = iter(refs)
    y_ref, ab_ref, arow_ref, dtrow_ref, s0_ref = next(it), next(it), next(it), next(it), next(it)
    oin_ref = next(it) if accumulate else None
    o_ref = next(it) if with_q else None
    sfin_ref = None if with_q else next(it)
    s_scr = next(it)

    c = GDN_CHUNK
    i = pl.program_id(1)
    ncol = y_ref.shape[1]
    koff = ncol - 2 * D_MODEL
    voff = ncol - D_MODEL

    @pl.when(i == 0)
    def _():
        s_scr[...] = s0_ref[...]

    ab = ab_ref[...]
    g_cmp = -jnp.exp(arow_ref[...]) * _softplus(ab + dtrow_ref[...])
    beta_cmp = _sigmoid(ab)
    ti = lax.broadcasted_iota(jnp.int32, (tg, tg), 0)
    tj = lax.broadcasted_iota(jnp.int32, (tg, tg), 1)
    cs = c.bit_length() - 1
    same_chunk = (ti >> cs) == (tj >> cs)
    order = (tj >= ti) if reverse else (tj <= ti)
    cum_mat = jnp.where(same_chunk & order, 1.0, 0.0).astype(BF16)
    gcum3 = jnp.dot(cum_mat, jnp.concatenate(_split_bf16(g_cmp, 3), axis=1), preferred_element_type=F32)
    gcum_cmp = gcum3[:, 0:LANES] + gcum3[:, LANES:2 * LANES] + gcum3[:, 2 * LANES:3 * LANES]
    gcum_t = gcum_cmp.T
    eg_cmp = jnp.exp(gcum_cmp)
    dir_off = HEADS if reverse else 0

    ii = lax.broadcasted_iota(jnp.int32, (c, c), 0)
    jj = lax.broadcasted_iota(jnp.int32, (c, c), 1)
    eye = jnp.where(ii == jj, 1.0, 0.0).astype(F32)
    incl = (ii <= jj) if reverse else (ii >= jj)
    strict = (ii < jj) if reverse else (ii > jj)

    n_chunks = tg // c
    chunk_order = list(range(n_chunks - 1, -1, -1) if reverse else range(n_chunks))
    inst = [(cidx, h) for cidx in chunk_order for h in range(HEADS)]

    knb, qsb, be, gc, eg, decay = [], [], [], [], [], []
    for cidx, h in inst:
        rs = slice(cidx * c, (cidx + 1) * c)
        knb.append(y_ref[rs, koff + h * HEAD_DIM:koff + (h + 1) * HEAD_DIM])
        if with_q:
            qsb.append(y_ref[rs, h * HEAD_DIM:(h + 1) * HEAD_DIM])
        la = dir_off + h
        lb = 2 * HEADS + dir_off + h
        be.append(jnp.broadcast_to(beta_cmp[rs, lb:lb + 1], (c, HEAD_DIM)))
        g_col = jnp.broadcast_to(gcum_cmp[rs, la:la + 1], (c, HEAD_DIM))
        g_row = gcum_t[la:la + 1, cidx * c:(cidx + 1) * c]
        gc.append(g_col)
        decay.append(jnp.exp(jnp.where(incl, g_col - g_row, NEG_BIG)))
        eg.append(jnp.broadcast_to(eg_cmp[rs, la:la + 1], (c, HEAD_DIM)))
    kn = [k.astype(F32) for k in knb]
    kb = [k * b for k, b in zip(kn, be)]
    nt_dims = (((1,), (1,)), ((), ()))
    if with_q:
        qs = [q.astype(F32) for q in qsb]
        kk = [lax.dot_general(jnp.concatenate([a.astype(BF16), q], axis=0), k, nt_dims,
                              preferred_element_type=F32) for a, q, k in zip(kb, qsb, knb)]
        attn = [(x[c:2 * c] * d).astype(BF16) for x, d in zip(kk, decay)]
    else:
        kk = [lax.dot_general(a.astype(BF16), k, nt_dims, preferred_element_type=F32)
              for a, k in zip(kb, knb)]
    a_mat = [jnp.where(strict, x[0:c] * d, 0.0) for x, d in zip(kk, decay)]
    t_inv = _unit_tri_inverse(a_mat, eye, ii, jj, reverse)
    sol = []
    for n, (cidx, h) in enumerate(inst):
        rs = slice(cidx * c, (cidx + 1) * c)
        vv = y_ref[rs, voff + h * HEAD_DIM:voff + (h + 1) * HEAD_DIM].astype(F32)
        rhs = jnp.concatenate([vv * be[n], kb[n] * eg[n]], axis=1)
        sol.append(_bdot(t_inv[n], rhs))

    for ci in range(n_chunks):
        base = ci * HEADS
        idx = range(base, base + HEADS)
        rs = slice(chunk_order[ci] * c, (chunk_order[ci] + 1) * c)
        s_old = [s_scr[h] for h in range(HEADS)]
        s_b = [s.astype(BF16) for s in s_old]
        if with_q:
            ws = [jnp.dot(jnp.concatenate([sol[n][:, HEAD_DIM:], qs[n] * eg[n]], axis=0).astype(BF16), sb,
                          preferred_element_type=F32) for n, sb in zip(idx, s_b)]
        else:
            ws = [jnp.dot(sol[n][:, HEAD_DIM:].astype(BF16), sb, preferred_element_type=F32)
                  for n, sb in zip(idx, s_b)]
        v_new = [(sol[n][:, 0:HEAD_DIM] - w[0:c]).astype(BF16) for n, w in zip(idx, ws)]
        last = rs.start if reverse else rs.stop - 1
        gl_cmp = gcum_cmp[last:last + 1, :]
        tail_cmp = jnp.exp(gl_cmp - gcum_cmp[rs, :])
        egl_cmp = jnp.exp(gl_cmp)
        k_tail = [(kn[n] * jnp.broadcast_to(tail_cmp[:, dir_off + h:dir_off + h + 1], (c, HEAD_DIM))).T
                  .astype(BF16) for h, n in enumerate(idx)]
        if with_q:
            kva = [jnp.dot(jnp.concatenate([kt, attn[n]], axis=0), v, preferred_element_type=F32)
                   for kt, n, v in zip(k_tail, idx, v_new)]
        else:
            kva = [jnp.dot(kt, v, preferred_element_type=F32) for kt, v in zip(k_tail, v_new)]
        for h in range(HEADS):
            s_decay = jnp.broadcast_to(egl_cmp[:, dir_off + h:dir_off + h + 1], (HEAD_DIM, HEAD_DIM))
            s_scr[h] = s_old[h] * s_decay + kva[h][0:HEAD_DIM]
        if with_q:
            for h in range(HEADS):
                hs = slice(h * HEAD_DIM, (h + 1) * HEAD_DIM)
                o = ws[h][c:2 * c] + kva[h][HEAD_DIM:HEAD_DIM + c]
                if accumulate:
                    o = o + oin_ref[rs, hs]
                o_ref[rs, hs] = o

    if not with_q:
        @pl.when(i == n_t - 1)
        def _():
            sfin_ref[...] = s_scr[...]


def _gdn_scan(y, ab, a_row, dt_row, s0, o_in, *, batch, reverse, with_q, tg):
    rows, ncol = y.shape
    n_t = rows // batch // tg
    accumulate = o_in is not None

    def tile(b, i):
        return (b * n_t + ((n_t - 1 - i) if reverse else i), 0)

    state = lambda b, i: (b, 0, 0, 0)
    const = lambda b, i: (0, 0)
    in_specs = [
        pl.BlockSpec((tg, ncol), tile),
        pl.BlockSpec((tg, LANES), tile),
        pl.BlockSpec((1, LANES), const),
        pl.BlockSpec((1, LANES), const),
        pl.BlockSpec((None, HEADS, HEAD_DIM, HEAD_DIM), state),
    ]
    args = [y, ab, a_row, dt_row, s0]
    if accumulate:
        in_specs.append(pl.BlockSpec((tg, D_MODEL), tile))
        args.append(o_in)
    if with_q:
        out_specs = pl.BlockSpec((tg, D_MODEL), tile)
        out_shape = jax.ShapeDtypeStruct((rows, D_MODEL), F32)
    else:
        out_specs = pl.BlockSpec((None, HEADS, HEAD_DIM, HEAD_DIM), state)
        out_shape = jax.ShapeDtypeStruct((batch, HEADS, HEAD_DIM, HEAD_DIM), F32)
    kern = functools.partial(_gdn_kernel, reverse=reverse, with_q=with_q, accumulate=accumulate,
                             n_t=n_t, tg=tg)
    return pl.pallas_call(
        kern,
        grid=(batch, n_t),
        in_specs=in_specs,
        out_specs=out_specs,
        out_shape=out_shape,
        scratch_shapes=[pltpu.VMEM((HEADS, HEAD_DIM, HEAD_DIM), F32)],
        compiler_params=pltpu.CompilerParams(dimension_semantics=("arbitrary", "arbitrary"),
                                             vmem_limit_bytes=VMEM_LIMIT),
        name="gdn_bwd" if reverse else "gdn_fwd",
    )(*args)


def _brancha_kernel(x_ref, mod_ref, gpre_ref, wa_ref, lng_ref, lnb_ref, wsp_ref, bsp_ref, wpa_ref, ma_ref):
    d = D_MODEL
    tm = x_ref.shape[0]
    hb = _prenorm(x_ref[...], gpre_ref[...], mod_ref[...]).astype(BF16)
    vg = _gelu_tanh(jnp.dot(hb, wa_ref[:, d:2 * d], preferred_element_type=F32))
    mu = jnp.mean(vg, axis=-1, keepdims=True)
    vc = vg - mu
    var = jnp.mean(vc * vc, axis=-1, keepdims=True)
    v = (vc * lax.rsqrt(var + EPS) * lng_ref[...] + lnb_ref[...]).astype(BF16)
    n_sp = tm // GM_CHUNK
    n_grp = d // HEAD_DIM
    mixed = []
    for g in range(n_grp):
        rhs = jnp.concatenate([v[n * GM_CHUNK:(n + 1) * GM_CHUNK, g * HEAD_DIM:(g + 1) * HEAD_DIM]
                               for n in range(n_sp)], axis=1)
        mixed.append(jnp.dot(wsp_ref[g], rhs, preferred_element_type=F32))
    s = jnp.concatenate(
        [jnp.concatenate([mixed[g][:, n * HEAD_DIM:(n + 1) * HEAD_DIM] for g in range(n_grp)], axis=1)
         + bsp_ref[...] for n in range(n_sp)], axis=0)
    u = _gelu_tanh(jnp.dot(hb, wa_ref[:, 0:d], preferred_element_type=F32))
    z = _silu(jnp.dot(hb, wa_ref[:, 2 * d:3 * d], preferred_element_type=F32))
    ya = (u * s * z).astype(BF16)
    gate = _sigmoid(jnp.dot(hb, wa_ref[:, 3 * d:4 * d], preferred_element_type=F32))
    ma_ref[...] = gate * jnp.dot(ya, wpa_ref[...], preferred_element_type=F32)


def _branch_a(x2d, mod3, tiles_per_mod, g_pre, w_pack, ln_g, ln_b, w_sp, b_sp_e, w_pa, tm):
    rows, d = x2d.shape
    const = lambda i: (0, 0)
    return pl.pallas_call(
        _brancha_kernel,
        grid=(rows // tm,),
        in_specs=[pl.BlockSpec((tm, d), lambda i: (i, 0)),
                  pl.BlockSpec((None, 1, 3 * d), lambda i: (i // tiles_per_mod, 0, 0)),
                  pl.BlockSpec((1, d), const),
                  pl.BlockSpec((d, 4 * d), lambda i: (0, 1)),
                  pl.BlockSpec((1, d), const),
                  pl.BlockSpec((1, d), const),
                  pl.BlockSpec(w_sp.shape, lambda i: (0, 0, 0)),
                  pl.BlockSpec((GM_CHUNK, d), const),
                  pl.BlockSpec((d, d), const)],
        out_specs=pl.BlockSpec((tm, d), lambda i: (i, 0)),
        out_shape=jax.ShapeDtypeStruct((rows, d), F32),
        compiler_params=pltpu.CompilerParams(vmem_limit_bytes=VMEM_LIMIT),
        name="brancha",
    )(x2d, mod3, g_pre, w_pack, ln_g, ln_b, w_sp, b_sp_e, w_pa)


def _out_kernel(x_ref, mod_ref, gpre_ref, gpost_ref, o_ref, ma_ref, wzb_ref, wgb_ref, wpb_ref, wout_ref,
                gon_ref, out_ref):
    d = D_MODEL
    x = x_ref[...]
    mod = mod_ref[...]
    hb = _prenorm(x, gpre_ref[...], mod).astype(BF16)
    zb = _silu(jnp.dot(hb, wzb_ref[...], preferred_element_type=F32))
    gon = gon_ref[...]
    ohs = [o_ref[:, h * HEAD_DIM:(h + 1) * HEAD_DIM] for h in range(HEADS)]
    ms = jnp.concatenate([jnp.mean(oh * oh, axis=-1, keepdims=True) for oh in ohs], axis=1)
    inv = lax.rsqrt(ms + EPS)
    parts = [oh * inv[:, h:h + 1] * gon for h, oh in enumerate(ohs)]
    yb = (jnp.concatenate(parts, axis=1) * zb).astype(BF16)
    gate_b = _sigmoid(jnp.dot(hb, wgb_ref[...], preferred_element_type=F32))
    merged = ma_ref[...] + gate_b * jnp.dot(yb, wpb_ref[...], preferred_element_type=F32)
    z = jnp.dot(merged.astype(BF16), wout_ref[...], preferred_element_type=F32)
    zn = z * lax.rsqrt(jnp.mean(z * z, axis=-1, keepdims=True) + EPS) * gpost_ref[...]
    out_ref[...] = x + mod[:, 2 * d:3 * d] * zn


def _merge_out(x2d, mod3, tiles_per_mod, g_pre, g_post, o, m_a, w_pack, w_pb, w_out, g_onorm, tm):
    rows, d = x2d.shape
    const = lambda i: (0, 0)
    row = lambda i: (i, 0)
    return pl.pallas_call(
        _out_kernel,
        grid=(rows // tm,),
        in_specs=[pl.BlockSpec((tm, d), row),
                  pl.BlockSpec((None, 1, 3 * d), lambda i: (i // tiles_per_mod, 0, 0)),
                  pl.BlockSpec((1, d), const),
                  pl.BlockSpec((1, d), const),
                  pl.BlockSpec((tm, d), row),
                  pl.BlockSpec((tm, d), row),
                  pl.BlockSpec((d, d), lambda i: (0, 3)),
                  pl.BlockSpec((d, d), lambda i: (0, 8)),
                  pl.BlockSpec((d, d), const),
                  pl.BlockSpec((d, d), const),
                  pl.BlockSpec((1, HEAD_DIM), const)],
        out_specs=pl.BlockSpec((tm, d), row),
        out_shape=jax.ShapeDtypeStruct((rows, d), F32),
        compiler_params=pltpu.CompilerParams(vmem_limit_bytes=VMEM_LIMIT),
        name="merge_out",
    )(x2d, mod3, g_pre, g_post, o, m_a, w_pack, w_pack, w_pb, w_out, g_onorm)


def _layer(x, ctx, mod_x, mod_c, g_pre, g_post, w_in, w_conv, a_log, dt_bias, g_onorm, gm_ln_g, gm_ln_b,
           w_sp, b_sp, w_pa, w_pb, w_out):
    batch, seq, d = x.shape
    ctx_len = ctx.shape[1]
    kd = HEADS * HEAD_DIM
    off_a = 3 * kd
    off_zb = off_a + 4 * HEADS

    tn = 1024
    w_t = jnp.transpose(w_in)
    w_pack = _repack_weights(w_t, list(range(0, off_a, tn)) + list(range(off_zb, w_in.shape[1], tn)), tn)
    w_ab2 = _repack_logit_weights(w_t, off_a, off_zb - off_a)
    a_row = jnp.pad(a_log.reshape(1, 2 * HEADS), ((0, 0), (0, LANES - 2 * HEADS)))
    dt_row = jnp.pad(dt_bias.reshape(1, 2 * HEADS), ((0, 0), (0, LANES - 2 * HEADS)))
    b_sp_e = jnp.repeat(b_sp.T, HEAD_DIM, axis=1)
    g_pre2 = g_pre.reshape(1, d)

    x2d = x.reshape(batch * seq, d)
    c2d = ctx.reshape(batch * ctx_len, d)
    tm = 1024

    kv_c, ab_c = _project_qkv(c2d, mod_c, 1 << 30, g_pre2, w_pack, w_ab2, w_conv, ctx_len, ctx_len, ("k", "v"))
    s_zero = jnp.zeros((batch, HEADS, HEAD_DIM, HEAD_DIM), F32)
    s_fwd = _gdn_scan(kv_c, ab_c, a_row, dt_row, s_zero, None,
                      batch=batch, reverse=False, with_q=False, tg=ctx_len)
    s_bwd = _gdn_scan(kv_c, ab_c, a_row, dt_row, s_zero, None,
                      batch=batch, reverse=True, with_q=False, tg=ctx_len)

    qkv, ab = _project_qkv(x2d, mod_x, seq // tm, g_pre2, w_pack, w_ab2, w_conv, tm, seq, ("q", "k", "v"))
    tg = 2 * GDN_CHUNK
    o_f = _gdn_scan(qkv, ab, a_row, dt_row, s_fwd, None, batch=batch, reverse=False, with_q=True, tg=tg)
    o = _gdn_scan(qkv, ab, a_row, dt_row, s_bwd, o_f, batch=batch, reverse=True, with_q=True, tg=tg)
    m_a = _branch_a(x2d, mod_x, seq // tm, g_pre2, w_pack, gm_ln_g.reshape(1, d), gm_ln_b.reshape(1, d),
                    w_sp.astype(BF16), b_sp_e, w_pa.astype(BF16), tm)
    out = _merge_out(x2d, mod_x, seq // tm, g_pre2, g_post.reshape(1, d), o, m_a, w_pack,
                     w_pb.astype(BF16), w_out.astype(BF16), g_onorm.reshape(1, HEAD_DIM), tm)
    return out.reshape(batch, seq, d)


def kernel(x, c, ctx, c_ctx, w_mod, b_mod, g_pre, g_post, w_in, w_conv, a_log, dt_bias, g_onorm, gm_ln_g,
           gm_ln_b, w_sp, b_sp, w_pa, w_pb, w_out):
    batch, _, d = x.shape
    depth = w_mod.shape[0]
    assert depth == 1, "context tokens are only read (never updated) by a single-layer stack"
    assert ctx.shape[1] % GDN_CHUNK == 0 and x.shape[1] % (8 * GDN_CHUNK) == 0 and d == D_MODEL
    pad = (-(batch + 1)) % 8
    conds = jnp.concatenate([c, c_ctx[None, :], jnp.zeros((pad, d), F32)], axis=0)
    i = 0
    m = _modulation(conds, w_mod[i], b_mod[i])
    mod_x = m[0:batch].reshape(batch, 1, 3 * d)
    mod_c = m[batch:batch + 1].reshape(1, 1, 3 * d)
    return _layer(x, ctx, mod_x, mod_c, g_pre[i], g_post[i], w_in[i], w_conv[i], a_log[i], dt_bias[i],
                  g_onorm[i], gm_ln_g[i], gm_ln_b[i], w_sp[i], b_sp[i], w_pa[i], w_pb[i], w_out[i])
```
